```python
import math
import jax, jax.numpy as jnp
from jax import lax
import numpy as np

D_MODEL = 2048
BATCH = 2
SEQ = 8192
DEPTH = 4

CHUNK = 64
N_MEM = 256
N_MIXERS = 2
N_A = (DEPTH + 1) // 2
N_B = DEPTH // 2
HEAD_DIM = 128
MEM_W = D_MODEL // 4
MEM_HEADS = MEM_W // HEAD_DIM
TOK_W = D_MODEL - MEM_W
SB_HEADS = TOK_W // HEAD_DIM
SB_BLOCK = 128
S5_GROUP = 16
S5_GROUPS = TOK_W // S5_GROUP
S5_STATE = 64
FFN_DIM = 256 * math.ceil(8 * D_MODEL / 3 / 256)
EPS = 1e-6

kernel_name = "hybrid_stickbreak_s5_macaron_memory_trunk"


def rms_norm(x, g):
    x32 = x.astype(jnp.float32)
    y = x32 * lax.rsqrt(jnp.mean(x32 * x32, axis=-1, keepdims=True) + EPS) * g.astype(jnp.float32)
    return y.astype(x.dtype)


def swiglu_ffn(h, w_gu, w_down):
    g, u = jnp.split(h @ w_gu, 2, axis=-1)
    return (jax.nn.silu(g) * u) @ w_down


def stick_breaking_attention(q, k, v):
    B, L, H, d = q.shape
    q, k, v = (a.transpose(0, 2, 1, 3) for a in (q, k, v))
    scale = 1.0 / math.sqrt(d)
    outs = []
    for qs in range(0, L, SB_BLOCK):
        qe = qs + SB_BLOCK
        z = jnp.einsum("bhqd,bhkd->bhqk", q[:, :, qs:qe], k[:, :, :qe]).astype(jnp.float32) * scale
        t_idx = qs + jnp.arange(SB_BLOCK)[:, None]
        s_idx = jnp.arange(qe)[None, :]
        before = s_idx < t_idx
        log_not = jnp.where(before, jax.nn.log_sigmoid(-z), 0.0)
        between = lax.cumsum(log_not, axis=3, reverse=True) - log_not
        w = jnp.where(before, jnp.exp(jax.nn.log_sigmoid(z) + between), 0.0)
        outs.append(jnp.einsum("bhqk,bhkd->bhqd", w.astype(v.dtype), v[:, :, :qe]))
    o = jnp.concatenate(outs, axis=2)
    return o.transpose(0, 2, 1, 3).reshape(B, L, H * d)


def _ssm_combine(left, right):
    a_l, b_l = left
    a_r, b_r = right
    return (a_r * a_l, a_r * b_l + b_r)


def s5_glu(u, log_dt, a_re, a_im, b_re, b_im, c_re, c_im, d_skip, w_glu):
    B, L, _ = u.shape
    f32 = jnp.float32
    dt = jnp.exp(log_dt.astype(f32))[:, None]
    lam = lax.complex(a_re.astype(f32), a_im.astype(f32))
    a_bar = jnp.exp(lam * dt)
    b = lax.complex(b_re.astype(f32), b_im.astype(f32))
    b_bar = ((a_bar - 1.0) / lam)[..., None] * b
    c = lax.complex(c_re.astype(f32), c_im.astype(f32))
    uc = u.astype(f32).reshape(B, L, S5_GROUPS, S5_GROUP)
    bu = jnp.einsum("gnc,blgc->blgn", b_bar, uc.astype(jnp.complex64))
    a_elems = jnp.broadcast_to(a_bar[None, None], bu.shape)
    _, states = lax.associative_scan(_ssm_combine, (a_elems, bu), axis=1)
    y = jnp.einsum("gcn,blgn->blgc", c, states).real + d_skip.astype(f32).reshape(S5_GROUPS, S5_GROUP) * uc
    y = jax.nn.gelu(y.reshape(B, L, TOK_W))
    y = y * jax.nn.sigmoid(y @ w_glu.astype(f32))
    return y.astype(u.dtype)


def memory_cross_attention(q, mem_h, w_mem_kv, q_gain, k_gain):
    B, L, _ = q.shape
    M = mem_h.shape[1]
    k, v = jnp.split(mem_h @ w_mem_kv, 2, axis=-1)
    q = rms_norm(q.reshape(B, L, MEM_HEADS, HEAD_DIM), q_gain)
    k = rms_norm(k.reshape(B, M, MEM_HEADS, HEAD_DIM), k_gain)
    v = v.reshape(B, M, MEM_HEADS, HEAD_DIM)
    s = jnp.einsum("blhd,bmhd->bhlm", q, k).astype(jnp.float32) / math.sqrt(HEAD_DIM)
    p = jax.nn.softmax(s, axis=-1).astype(v.dtype)
    o = jnp.einsum("bhlm,bmhd->blhd", p, v)
    return o.reshape(B, L, MEM_W)


def setup_inputs(seed: int = 0) -> dict:
    key = jax.random.key(seed)
    ks = jax.random.split(key, 32)
    f32 = jnp.float32

    def nrm(k, shape, fan_in):
        return jax.random.normal(k, shape, f32) * fan_in ** -0.5

    def gain(k, shape):
        return 1.0 + 0.02 * jax.random.normal(k, shape, f32)

    G, N, C = S5_GROUPS, S5_STATE, S5_GROUP
    return dict(
        x=jax.random.normal(ks[0], (BATCH, SEQ, D_MODEL), f32),
        mem=jax.random.normal(ks[1], (BATCH, N_MEM, D_MODEL), f32),
        ffn1_norm=gain(ks[2], (DEPTH, D_MODEL)),
        ffn1_w_gu=nrm(ks[3], (DEPTH, D_MODEL, 2 * FFN_DIM), D_MODEL),
        ffn1_w_down=nrm(ks[4], (DEPTH, FFN_DIM, D_MODEL), FFN_DIM),
        mix_norm=gain(ks[5], (DEPTH, D_MODEL)),
        mem_norm=gain(ks[6], (DEPTH, D_MODEL)),
        w_mem_kv=nrm(ks[7], (DEPTH, D_MODEL, 2 * MEM_W), D_MODEL),
        xq_norm=gain(ks[8], (DEPTH, HEAD_DIM)),
        xk_norm=gain(ks[9], (DEPTH, HEAD_DIM)),
        w_out=nrm(ks[10], (DEPTH, TOK_W + MEM_W, D_MODEL), TOK_W + MEM_W),
        ffn2_norm=gain(ks[11], (DEPTH, D_MODEL)),
        ffn2_w_gu=nrm(ks[12], (DEPTH, D_MODEL, 2 * FFN_DIM), D_MODEL),
        ffn2_w_down=nrm(ks[13], (DEPTH, FFN_DIM, D_MODEL), FFN_DIM),
        sb_w_in=nrm(ks[14], (N_A, D_MODEL, 3 * TOK_W + MEM_W), D_MODEL),
        s5_w_in=nrm(ks[15], (N_B, D_MODEL, TOK_W + MEM_W), D_MODEL),
        s5_log_dt=jax.random.uniform(ks[16], (N_B, G), f32, math.log(1e-3), math.log(1e-1)),
        s5_a_re=-0.5 + 0.01 * jax.random.normal(ks[17], (N_B, G, N), f32),
        s5_a_im=jnp.pi * jnp.arange(N, dtype=f32) + 0.01 * jax.random.normal(ks[18], (N_B, G, N), f32),
        s5_b_re=nrm(ks[19], (N_B, G, N, C), 2 * C),
        s5_b_im=nrm(ks[20], (N_B, G, N, C), 2 * C),
        s5_c_re=nrm(ks[21], (N_B, G, C, N), 2 * N),
        s5_c_im=nrm(ks[22], (N_B, G, C, N), 2 * N),
        s5_d=jax.random.normal(ks[23], (N_B, TOK_W), f32),
        s5_w_glu=nrm(ks[24], (N_B, TOK_W, TOK_W), TOK_W),
    )


def reference(x, mem, ffn1_norm, ffn1_w_gu, ffn1_w_down, mix_norm, mem_norm, w_mem_kv,
              xq_norm, xk_norm, w_out, ffn2_norm, ffn2_w_gu, ffn2_w_down, sb_w_in,
              s5_w_in, s5_log_dt, s5_a_re, s5_a_im, s5_b_re, s5_b_im, s5_c_re, s5_c_im,
              s5_d, s5_w_glu):
    B, L, _ = x.shape
    for i in range(DEPTH):
        x = x + 0.5 * swiglu_ffn(rms_norm(x, ffn1_norm[i]), ffn1_w_gu[i], ffn1_w_down[i])
        h = rms_norm(x, mix_norm[i])
        j = i // N_MIXERS
        if i % N_MIXERS == 0:
            q, k, v, q_mem = jnp.split(h @ sb_w_in[j], [TOK_W, 2 * TOK_W, 3 * TOK_W], axis=-1)
            shp = (B, L, SB_HEADS, HEAD_DIM)
            tok = stick_breaking_attention(q.reshape(shp), k.reshape(shp), v.reshape(shp))
        else:
            u, q_mem = jnp.split(h @ s5_w_in[j], [TOK_W], axis=-1)
            tok = s5_glu(u, s5_log_dt[j], s5_a_re[j], s5_a_im[j], s5_b_re[j], s5_b_im[j],
                         s5_c_re[j], s5_c_im[j], s5_d[j], s5_w_glu[j])
        mem_h = rms_norm(mem, mem_norm[i])
        cross = memory_cross_attention(q_mem, mem_h, w_mem_kv[i], xq_norm[i], xk_norm[i])
        x = x + jnp.concatenate([tok, cross], axis=-1) @ w_out[i]
        x = x + 0.5 * swiglu_ffn(rms_norm(x, ffn2_norm[i]), ffn2_w_gu[i], ffn2_w_down[i])
    return x
```

```python
import functools
import math

import jax
import jax.numpy as jnp
from jax import lax
from jax.experimental import pallas as pl
from jax.experimental.pallas import tpu as pltpu

F32 = jnp.float32
BF16 = jnp.bfloat16

EPS = 1e-6
HEAD_DIM = 128
MEM_HEADS = 4
MEM_W = MEM_HEADS * HEAD_DIM
S5_GROUP = 16
S5_STATE = 64
S5_CHUNK = 64

V7X_VMEM_LIMIT_BYTES = 56 * 1024 * 1024


def _cparams(*sem):
    return pltpu.CompilerParams(dimension_semantics=sem, vmem_limit_bytes=V7X_VMEM_LIMIT_BYTES)


def _rms(x, gain):
    return x * lax.rsqrt(jnp.mean(x * x, axis=-1, keepdims=True) + EPS) * gain


def _dot(a, b):
    return jnp.dot(a, b, preferred_element_type=F32)


def _dot_nt(a, b):
    return lax.dot_general(a, b, (((1,), (1,)), ((), ())), preferred_element_type=F32)


def _split3(x):
    h = x.astype(BF16)
    r = x - h.astype(F32)
    m = r.astype(BF16)
    l = (r - m.astype(F32)).astype(BF16)
    return h, m, l


def _dot_sel(x, sel):
    h, m, l = _split3(x)
    return _dot(h, sel) + _dot(m, sel) + _dot(l, sel)


def _dot3(a, b):
    ah = a.astype(BF16)
    al = (a - ah.astype(F32)).astype(BF16)
    bh = b.astype(BF16)
    bl = (b - bh.astype(F32)).astype(BF16)
    return _dot(ah, bh) + _dot(ah, bl) + _dot(al, bh)


def _ffn_body(x_ref, gain_ref, wg_ref, wu_ref, wd_ref, o_ref, h_ref):
    j = pl.program_id(1)

    @pl.when(j == 0)
    def _():
        x = x_ref[...]
        h_ref[...] = _rms(x, gain_ref[...]).astype(BF16)
        o_ref[...] = x

    h = h_ref[...]
    g = _dot(h, wg_ref[...])
    u = _dot(h, wu_ref[...])
    a = (g * jax.nn.sigmoid(g) * (u * 0.5)).astype(BF16)
    o_ref[...] += _dot(a, wd_ref[...])


def _ffn(x, gain, w_gu, w_down, layer, *, tm, tf):
    m, d = x.shape
    f = w_down.shape[1]
    nf = f // tf
    return pl.pallas_call(
        _ffn_body,
        grid=(m // tm, nf),
        in_specs=[
            pl.BlockSpec((tm, d), lambda i, j: (i, 0)),
            pl.BlockSpec((None, 1, d), lambda i, j: (layer, 0, 0)),
            pl.BlockSpec((None, d, tf), lambda i, j: (layer, 0, j)),
            pl.BlockSpec((None, d, tf), lambda i, j: (layer, 0, nf + j)),
            pl.BlockSpec((None, tf, d), lambda i, j: (layer, j, 0)),
        ],
        out_specs=pl.BlockSpec((tm, d), lambda i, j: (i, 0)),
        out_shape=jax.ShapeDtypeStruct((m, d), F32),
        scratch_shapes=[pltpu.VMEM((tm, d), BF16)],
        compiler_params=_cparams("parallel", "arbitrary"),
        name="ffn",
    )(x, gain, w_gu, w_gu, w_down)


def _sb_inproj_body(x_ref, gain_ref, w_ref, o_ref, h_ref, *, n_q_blocks, q_scale):
    j = pl.program_id(1)

    @pl.when(j == 0)
    def _():
        h_ref[...] = _rms(x_ref[...], gain_ref[...]).astype(BF16)

    acc = _dot(h_ref[...], w_ref[...])
    scale = jnp.where(j < n_q_blocks, q_scale, 1.0).astype(F32)
    o_ref[...] = (acc * scale).astype(o_ref.dtype)


def _sb_inproj(x, gain, w_in, layer, j_layer, *, tm, tn, tok_w):
    m, d = x.shape
    n = w_in.shape[2]
    body = functools.partial(_sb_inproj_body, n_q_blocks=tok_w // tn,
                             q_scale=1.0 / math.sqrt(HEAD_DIM))
    return pl.pallas_call(
        body,
        grid=(m // tm, n // tn),
        in_specs=[
            pl.BlockSpec((tm, d), lambda i, j: (i, 0)),
            pl.BlockSpec((None, 1, d), lambda i, j: (layer, 0, 0)),
            pl.BlockSpec((None, d, tn), lambda i, j: (j_layer, 0, j)),
        ],
        out_specs=pl.BlockSpec((tm, tn), lambda i, j: (i, j)),
        out_shape=jax.ShapeDtypeStruct((m, n), BF16),
        scratch_shapes=[pltpu.VMEM((tm, d), BF16)],
        compiler_params=_cparams("parallel", "arbitrary"),
        name="sb_inproj",
    )(x, gain, w_in)


def _sb_attn_body(q_ref, k_ref, v_ref, o_ref, acc_ref, carry_ref, *, blk):
    qi = pl.program_id(2)
    q = q_ref[...]
    row = lax.broadcasted_iota(jnp.int32, (blk, blk), 0)
    col = lax.broadcasted_iota(jnp.int32, (blk, blk), 1)
    suffix = (row > col).astype(BF16)
    before = col < row

    def visit(kb, diagonal):
        start = pl.multiple_of(kb * blk, blk)
        k = k_ref[pl.ds(start, blk), :]
        v = v_ref[pl.ds(start, blk), :]
        z = _dot_nt(q, k)
        lg = jnp.log(1.0 + jnp.exp(-jnp.abs(z)))
        sp = jnp.maximum(z, 0.0) + lg
        ls = jnp.minimum(z, 0.0) - lg
        if diagonal:
            sp = jnp.where(before, sp, 0.0)
        spb = sp.astype(BF16)
        inner = _dot(spb, suffix)
        total = inner[:, 0:1] + spb[:, 0:1].astype(F32)
        carry = carry_ref[...]
        carry2 = jnp.concatenate([carry] * (blk // HEAD_DIM), axis=1)
        w = jnp.exp(ls - inner - carry2)
        if diagonal:
            w = jnp.where(before, w, 0.0)
        acc_ref[...] += _dot(w.astype(BF16), v)
        carry_ref[...] = carry + jnp.broadcast_to(total, carry.shape)

    acc_ref[...] = jnp.zeros_like(acc_ref)
    carry_ref[...] = jnp.zeros_like(carry_ref)
    visit(qi, True)

    def step(i, _):
        visit(qi - 1 - i, False)
        return 0

    lax.fori_loop(0, qi, step, 0)
    o_ref[...] = acc_ref[...].astype(o_ref.dtype)


def _sb_attention(qkv, *, batch, seq, heads, blk):
    nq = seq // blk
    body = functools.partial(_sb_attn_body, blk=blk)
    return pl.pallas_call(
        body,
        grid=(batch, heads, nq),
        in_specs=[
            pl.BlockSpec((blk, HEAD_DIM), lambda b, h, i: (b * nq + i, h)),
            pl.BlockSpec((seq, HEAD_DIM), lambda b, h, i: (b, heads + h)),
            pl.BlockSpec((seq, HEAD_DIM), lambda b, h, i: (b, 2 * heads + h)),
        ],
        out_specs=pl.BlockSpec((blk, HEAD_DIM), lambda b, h, i: (b * nq + i, h)),
        out_shape=jax.ShapeDtypeStruct((batch * seq, heads * HEAD_DIM), BF16),
        scratch_shapes=[pltpu.VMEM((blk, HEAD_DIM), F32), pltpu.VMEM((blk, HEAD_DIM), F32)],
        compiler_params=_cparams("parallel", "parallel", "arbitrary"),
        name="sb_attention",
    )(qkv, qkv, qkv)


def _mem_kv_body(mem_ref, gain_ref, w_ref, kgain_ref, k_ref, v_ref):
    h = _rms(mem_ref[...], gain_ref[...]).astype(BF16)
    kv = _dot(h, w_ref[...])
    for hd in range(MEM_HEADS):
        sl = slice(hd * HEAD_DIM, (hd + 1) * HEAD_DIM)
        k_ref[:, sl] = _rms(kv[:, sl], kgain_ref[...]).astype(BF16)
    v_ref[...] = kv[:, MEM_W:].astype(BF16)


def _mem_kv(mem, gain, w_kv, kgain, layer):
    rows, d = mem.shape
    return pl.pallas_call(
        _mem_kv_body,
        grid=(1,),
        in_specs=[
            pl.BlockSpec((rows, d), lambda i: (0, 0)),
            pl.BlockSpec((None, 1, d), lambda i: (layer, 0, 0)),
            pl.BlockSpec((None, d, 2 * MEM_W), lambda i: (layer, 0, 0)),
            pl.BlockSpec((None, 1, HEAD_DIM), lambda i: (layer, 0, 0)),
        ],
        out_specs=[pl.BlockSpec((rows, MEM_W), lambda i: (0, 0))] * 2,
        out_shape=[jax.ShapeDtypeStruct((rows, MEM_W), BF16)] * 2,
        compiler_params=_cparams("arbitrary"),
        name="mem_kv",
    )(mem, gain, w_kv, kgain)


def _cross_attention(qm, k, v, qgain):
    outs = []
    for hd in range(MEM_HEADS):
        sl = slice(hd * HEAD_DIM, (hd + 1) * HEAD_DIM)
        qn = _rms(qm[:, sl], qgain).astype(BF16)
        s = _dot_nt(qn, k[:, sl]) * (1.0 / math.sqrt(HEAD_DIM))
        p = jnp.exp(s - jnp.max(s, axis=-1, keepdims=True))
        p = p / jnp.sum(p, axis=-1, keepdims=True)
        outs.append(_dot(p.astype(BF16), v[:, sl]))
    return jnp.concatenate(outs, axis=1)


def _sb_outproj_body(x_ref, tok_ref, qm_ref, k_ref, v_ref, qgain_ref, w1_ref, w2_ref, o_ref):
    cross = _cross_attention(qm_ref[...].astype(F32), k_ref[...], v_ref[...], qgain_ref[...])
    o_ref[...] = (x_ref[...] + _dot(tok_ref[...], w1_ref[...])
                  + _dot(cross.astype(BF16), w2_ref[...]))


def _sb_outproj(x, tok, qkv, kmem, vmem, qgain, w_out, layer, *, tm, seq, tok_w):
    m, d = x.shape
    n_mem = kmem.shape[0] // (m // seq)
    per_batch = seq // tm
    qm_block = tok_w * 3 // MEM_W
    return pl.pallas_call(
        _sb_outproj_body,
        grid=(m // tm,),
        in_specs=[
            pl.BlockSpec((tm, d), lambda i: (i, 0)),
            pl.BlockSpec((tm, tok_w), lambda i: (i, 0)),
            pl.BlockSpec((tm, MEM_W), lambda i: (i, qm_block)),
            pl.BlockSpec((n_mem, MEM_W), lambda i: (i // per_batch, 0)),
            pl.BlockSpec((n_mem, MEM_W), lambda i: (i // per_batch, 0)),
            pl.BlockSpec((None, 1, HEAD_DIM), lambda i: (layer, 0, 0)),
            pl.BlockSpec((None, tok_w, d), lambda i: (layer, 0, 0)),
            pl.BlockSpec((None, MEM_W, d), lambda i: (layer, tok_w // MEM_W, 0)),
        ],
        out_specs=pl.BlockSpec((tm, d), lambda i: (i, 0)),
        out_shape=jax.ShapeDtypeStruct((m, d), F32),
        compiler_params=_cparams("parallel"),
        name="sb_outproj",
    )(x, tok, qkv, kmem, vmem, qgain, w_out, w_out)


def _s5_inproj_body(x_ref, gain_ref, wut_ref, wq_ref, u_ref, qm_ref):
    b, nc, d = x_ref.shape
    h = _rms(x_ref[...].reshape(b * nc, d), gain_ref[...]).astype(BF16)
    ut = _dot_nt(wut_ref[...], h)
    u_ref[...] = ut.reshape(u_ref.shape)
    qm_ref[0] = _dot(h, wq_ref[...]).astype(qm_ref.dtype)


def _s5_inproj(x3, gain, wut, wq, layer, j_layer, *, d):
    b, nc, td = x3.shape
    t = td // d
    tok_w = wut.shape[1]
    groups = tok_w // S5_GROUP
    n = b * nc
    return pl.pallas_call(
        _s5_inproj_body,
        grid=(t,),
        in_specs=[
            pl.BlockSpec((b, nc, d), lambda i: (0, 0, i)),
            pl.BlockSpec((None, 1, d), lambda i: (layer, 0, 0)),
            pl.BlockSpec((None, tok_w, d), lambda i: (j_layer, 0, 0)),
            pl.BlockSpec((None, d, MEM_W), lambda i: (j_layer, 0, 0)),
        ],
        out_specs=[
            pl.BlockSpec((groups, 1, S5_GROUP, n), lambda i: (0, i, 0, 0)),
            pl.BlockSpec((1, n, MEM_W), lambda i: (i, 0, 0)),
        ],
        out_shape=[
            jax.ShapeDtypeStruct((groups, t, S5_GROUP, n), F32),
            jax.ShapeDtypeStruct((t, n, MEM_W), BF16),
        ],
        compiler_params=_cparams("parallel"),
        name="s5_inproj",
    )(x3, gain, wut, wq)


def _cexp(zr, zi, k):
    mag = jnp.exp(zr * k)
    return mag * jnp.cos(zi * k), mag * jnp.sin(zi * k)


def _s5_core_body(u_ref, ldt_ref, arc_ref, aic_ref, arr_ref, air_ref, br_ref, bi_ref,
                  cr_ref, ci_ref, d_ref, exp_ref, tile_ref, y_ref, mt_ref, q_ref, *, n_chunks):
    t = S5_CHUNK
    c = S5_GROUP
    ns = S5_STATE
    n = u_ref.shape[-1]
    dt = jnp.exp(ldt_ref[0])

    lr, li = arc_ref[0], aic_ref[0]
    zr, zi = lr * dt, li * dt
    abr, abi = _cexp(zr, zi, 1.0)
    nr, ni = abr - 1.0, abi
    den = lr * lr + li * li
    fr = (nr * lr + ni * li) / den
    fi = (ni * lr - nr * li) / den
    b_re, b_im = br_ref[0], bi_ref[0]
    bbr = fr * b_re - fi * b_im
    bbi = fr * b_im + fi * b_re

    back = (t - 1 - lax.broadcasted_iota(jnp.int32, (ns, t), 1)).astype(F32)
    pwr, pwi = _cexp(zr, zi, back)
    pwr_x, pwi_x = _dot_sel(pwr, exp_ref[...]), _dot_sel(pwi, exp_ref[...])
    bbr_x, bbi_x = _dot_sel(bbr, tile_ref[...]), _dot_sel(bbi, tile_ref[...])
    p_re = pwr_x * bbr_x - pwi_x * bbi_x
    p_im = pwr_x * bbi_x + pwi_x * bbr_x
    p = jnp.concatenate([p_re, p_im], axis=0)

    c_re, c_im = cr_ref[0], ci_ref[0]
    c_ext = jnp.concatenate([c_re, -c_im], axis=1)
    k_rev = jnp.concatenate([_dot3(c_ext, p), jnp.zeros((c, t * c), F32)], axis=1)
    width = 2 * t * c
    for step in range(t):
        off = (t - 1 - step) * c
        win = k_rev if off == 0 else pltpu.roll(k_rev, width - off, 1)
        mt_ref[step * c:(step + 1) * c, :] = win[:, :t * c].astype(BF16)

    zr_r, zi_r = arr_ref[0] * dt, air_ref[0] * dt
    fwd = (lax.broadcasted_iota(jnp.int32, (t, ns), 0) + 1).astype(F32)
    qpr, qpi = _cexp(zr_r, zi_r, fwd)
    for step in range(t):
        ar, ai = qpr[step:step + 1, :], qpi[step:step + 1, :]
        q_ref[step * c:(step + 1) * c, 0:ns] = (c_re * ar - c_im * ai).astype(BF16)
        q_ref[step * c:(step + 1) * c, ns:2 * ns] = (-(c_re * ai + c_im * ar)).astype(BF16)

    u = u_ref[0].reshape(t * c, n)
    ub = u.astype(BF16)
    inj = _dot(p.astype(BF16), ub)
    s_re, s_im = inj[:ns], inj[ns:]

    pos = lax.broadcasted_iota(jnp.int32, (ns, n), 1) % n_chunks
    mr, mi = _cexp(zr, zi, float(t))
    shift = 1
    while shift < n_chunks:
        keep = pos >= shift
        sh_re = jnp.where(keep, pltpu.roll(s_re, shift, 1), 0.0)
        sh_im = jnp.where(keep, pltpu.roll(s_im, shift, 1), 0.0)
        s_re, s_im = s_re + mr * sh_re - mi * sh_im, s_im + mr * sh_im + mi * sh_re
        mr, mi = mr * mr - mi * mi, 2.0 * mr * mi
        shift *= 2
    first = pos >= 1
    prev = jnp.concatenate([jnp.where(first, pltpu.roll(s_re, 1, 1), 0.0),
                            jnp.where(first, pltpu.roll(s_im, 1, 1), 0.0)], axis=0)

    dcol = jnp.concatenate([d_ref[0]] * t, axis=0)
    y = _dot(mt_ref[...], ub) + _dot(q_ref[...], prev.astype(BF16)) + dcol * u
    y_ref[0] = jax.nn.gelu(y).reshape(t, c, n)


def _s5_core(u4, ldt, arc, aic, arr, air, b_re, b_im, c_re, c_im, dcol, expand, tile, j_layer,
             *, n_chunks):
    groups, t, c, n = u4.shape
    ns = S5_STATE
    body = functools.partial(_s5_core_body, n_chunks=n_chunks)

    def per_group(shape):
        return pl.BlockSpec((None, 1) + shape, lambda g: (j_layer, g, 0, 0))

    return pl.pallas_call(
        body,
        grid=(groups,),
        in_specs=[
            pl.BlockSpec((1, t, c, n), lambda g: (g, 0, 0, 0)),
            per_group((1, 1)),
            per_group((ns, 1)), per_group((ns, 1)),
            per_group((1, ns)), per_group((1, ns)),
            per_group((ns, c)), per_group((ns, c)),
            per_group((c, ns)), per_group((c, ns)),
            per_group((c, 1)),
            pl.BlockSpec((t, t * c), lambda g: (0, 0)),
            pl.BlockSpec((c, t * c), lambda g: (0, 0)),
        ],
        out_specs=pl.BlockSpec((1, t, c, n), lambda g: (g, 0, 0, 0)),
        out_shape=jax.ShapeDtypeStruct((groups, t, c, n), F32),
        scratch_shapes=[pltpu.VMEM((t * c, t * c), BF16), pltpu.VMEM((t * c, 2 * ns), BF16)],
        compiler_params=_cparams("parallel"),
        name="s5_core",
    )(u4, ldt, arc, aic, arr, air, b_re, b_im, c_re, c_im, dcol, expand, tile)


def _s5_outproj_body(x_ref, y_ref, qm_ref, k_ref, v_ref, qgain_ref, wglut_ref, woutt_ref, o_ref,
                     *, n_mem):
    b, nc, d = x_ref.shape
    tok_w = wglut_ref.shape[0]
    yt = y_ref[...].reshape(tok_w, b * nc)
    gate = _dot(wglut_ref[...], yt.astype(BF16))
    tok_t = (yt * jax.nn.sigmoid(gate)).astype(BF16)
    qm = qm_ref[0].astype(F32)
    cross = jnp.concatenate(
        [_cross_attention(qm[i * nc:(i + 1) * nc], k_ref[i * n_mem:(i + 1) * n_mem],
                          v_ref[i * n_mem:(i + 1) * n_mem], qgain_ref[...]) for i in range(b)],
        axis=0)
    mixed_t = jnp.concatenate([tok_t, cross.T.astype(BF16)], axis=0)
    upd = _dot(woutt_ref[...], mixed_t).T
    o_ref[...] = x_ref[...] + upd.reshape(b, nc, d)


def _s5_outproj(x3, y4, qm, kmem, vmem, qgain, wglut, woutt, layer, j_layer, *, d):
    b, nc, td = x3.shape
    t = td // d
    groups, _, c, n = y4.shape
    tok_w = groups * c
    n_mem = kmem.shape[0] // b
    body = functools.partial(_s5_outproj_body, n_mem=n_mem)
    return pl.pallas_call(
        body,
        grid=(t,),
        in_specs=[
            pl.BlockSpec((b, nc, d), lambda i: (0, 0, i)),
            pl.BlockSpec((groups, 1, c, n), lambda i: (0, i, 0, 0)),
            pl.BlockSpec((1, n, MEM_W), lambda i: (i, 0, 0)),
            pl.BlockSpec((b * n_mem, MEM_W), lambda i: (0, 0)),
            pl.BlockSpec((b * n_mem, MEM_W), lambda i: (0, 0)),
            pl.BlockSpec((None, 1, HEAD_DIM), lambda i: (layer, 0, 0)),
            pl.BlockSpec((None, tok_w, tok_w), lambda i: (j_layer, 0, 0)),
            pl.BlockSpec((None, d, d), lambda i: (layer, 0, 0)),
        ],
        out_specs=pl.BlockSpec((b, nc, d), lambda i: (0, 0, i)),
        out_shape=jax.ShapeDtypeStruct(x3.shape, F32),
        compiler_params=_cparams("parallel"),
        name="s5_outproj",
    )(x3, y4, qm, kmem, vmem, qgain, wglut, woutt)


FFN_TM = 512
FFN_TF = 512
PROJ_TM = 1024
PROJ_TN = 512
OUT_TM = 512
SB_BLK = 256


def kernel(x, mem, ffn1_norm, ffn1_w_gu, ffn1_w_down, mix_norm, mem_norm, w_mem_kv, xq_norm,
           xk_norm, w_out, ffn2_norm, ffn2_w_gu, ffn2_w_down, sb_w_in, s5_w_in, s5_log_dt,
           s5_a_re, s5_a_im, s5_b_re, s5_b_im, s5_c_re, s5_c_im, s5_d, s5_w_glu):
    batch, seq, d = x.shape
    depth = ffn1_norm.shape[0]
    n_mem = mem.shape[1]
    tok_w = d - MEM_W
    heads = tok_w // HEAD_DIM
    groups = tok_w // S5_GROUP
    n_chunks = seq // S5_CHUNK
    n_b = s5_w_in.shape[0]
    ns, c, t = S5_STATE, S5_GROUP, S5_CHUNK

    def row(g):
        return g.reshape(g.shape[0], 1, g.shape[1])

    ffn1_norm, ffn2_norm, mix_norm, mem_norm = map(row, (ffn1_norm, ffn2_norm, mix_norm, mem_norm))
    xq_norm, xk_norm = row(xq_norm), row(xk_norm)
    w1_gu, w1_down = ffn1_w_gu.astype(BF16), ffn1_w_down.astype(BF16)
    w2_gu, w2_down = ffn2_w_gu.astype(BF16), ffn2_w_down.astype(BF16)
    w_kv = w_mem_kv.astype(BF16)
    w_out_b = w_out.astype(BF16)
    w_out_t = jnp.swapaxes(w_out, 1, 2).astype(BF16)
    sb_w = sb_w_in.astype(BF16)
    s5_wu_t = jnp.swapaxes(s5_w_in[:, :, :tok_w], 1, 2).astype(BF16)
    s5_wq = s5_w_in[:, :, tok_w:].astype(BF16)
    s5_wglu_t = jnp.swapaxes(s5_w_glu, 1, 2).astype(BF16)

    ldt = s5_log_dt.reshape(n_b, groups, 1, 1)
    arc, aic = s5_a_re.reshape(n_b, groups, ns, 1), s5_a_im.reshape(n_b, groups, ns, 1)
    arr, air = s5_a_re.reshape(n_b, groups, 1, ns), s5_a_im.reshape(n_b, groups, 1, ns)
    dcol = s5_d.reshape(n_b, groups, c, 1)
    lane = jnp.arange(t * c)
    expand = (lane[None, :] // c == jnp.arange(t)[:, None]).astype(BF16)
    tile = (lane[None, :] % c == jnp.arange(c)[:, None]).astype(BF16)

    mem2 = mem.reshape(batch * n_mem, d)
    xs = x.reshape(batch * seq, d)
    for layer in range(depth):
        xs = _ffn(xs, ffn1_norm, w1_gu, w1_down, layer, tm=FFN_TM, tf=FFN_TF)
        kmem, vmem = _mem_kv(mem2, mem_norm, w_kv, xk_norm, layer)
        j_layer = layer // 2
        if layer % 2 == 0:
            qkv = _sb_inproj(xs, mix_norm, sb_w, layer, j_layer, tm=PROJ_TM, tn=PROJ_TN,
                             tok_w=tok_w)
            tok = _sb_attention(qkv, batch=batch, seq=seq, heads=heads, blk=SB_BLK)
            xs = _sb_outproj(xs, tok, qkv, kmem, vmem, xq_norm, w_out_b, layer, tm=OUT_TM,
                             seq=seq, tok_w=tok_w)
        else:
            x3 = xs.reshape(batch, n_chunks, t * d)
            u4, qm = _s5_inproj(x3, mix_norm, s5_wu_t, s5_wq, layer, j_layer, d=d)
            y4 = _s5_core(u4, ldt, arc, aic, arr, air, s5_b_re, s5_b_im, s5_c_re, s5_c_im, dcol,
                          expand, tile, j_layer, n_chunks=n_chunks)
            x3 = _s5_outproj(x3, y4, qm, kmem, vmem, xq_norm, s5_wglu_t, w_out_t, layer, j_layer,
                             d=d)
            xs = x3.reshape(batch * seq, d)
        xs = _ffn(xs, ffn2_norm, w2_gu, w2_down, layer, tm=FFN_TM, tf=FFN_TF)
    return xs.reshape(batch, seq, d)
```

```python
import functools
import math

import jax
import jax.numpy as jnp
from jax import lax
from jax.experimental import pallas as pl
from jax.experimental.pallas import tpu as pltpu

F32 = jnp.float32
BF16 = jnp.bfloat16

EPS = 1e-6
HEAD_DIM = 128
MEM_HEADS = 4
MEM_W = MEM_HEADS * HEAD_DIM
S5_GROUP = 16
S5_STATE = 64
S5_CHUNK = 64

V7X_VMEM_LIMIT_BYTES = 56 * 1024 * 1024


def _cparams(*sem):
    return pltpu.CompilerParams(dimension_semantics=sem, vmem_limit_bytes=V7X_VMEM_LIMIT_BYTES)


def _rms(x, gain):
    return x * lax.rsqrt(jnp.mean(x * x, axis=-1, keepdims=True) + EPS) * gain


def _dot(a, b):
    return jnp.dot(a, b, preferred_element_type=F32)


def _dot_nt(a, b):
    return lax.dot_general(a, b, (((1,), (1,)), ((), ())), preferred_element_type=F32)


def _split3(x):
    h = x.astype(BF16)
    r = x - h.astype(F32)
    m = r.astype(BF16)
    l = (r - m.astype(F32)).astype(BF16)
    return h, m, l


def _dot_sel(x, sel):
    h, m, l = _split3(x)
    return _dot(h, sel) + _dot(m, sel) + _dot(l, sel)


def _dot3(a, b):
    ah = a.astype(BF16)
    al = (a - ah.astype(F32)).astype(BF16)
    bh = b.astype(BF16)
    bl = (b - bh.astype(F32)).astype(BF16)
    return _dot(ah, bh) + _dot(ah, bl) + _dot(al, bh)


def _ffn_body(x_ref, gain_ref, wg_ref, wu_ref, wd_ref, o_ref, h_ref, *maybe_acc_ref, layout):
    j = pl.program_id(1)
    tm, d = h_ref.shape
    acc = maybe_acc_ref[0] if layout == "from_chunked" else o_ref

    @pl.when(j == 0)
    def _():
        if layout == "to_chunked":
            t, cpb, _ = acc.shape
            acc[...] = jnp.swapaxes(x_ref[...].reshape(cpb, t, d), 0, 1)
        else:
            acc[...] = x_ref[...]
        x = acc[...].reshape(tm, d)
        h_ref[...] = _rms(x, gain_ref[...]).astype(BF16)

    h = h_ref[...]
    g = _dot(h, wg_ref[...])
    u = _dot(h, wu_ref[...])
    a = (g * jax.nn.sigmoid(g) * (u * 0.5)).astype(BF16)
    acc[...] += _dot(a, wd_ref[...]).reshape(acc.shape)

    if layout == "from_chunked":
        @pl.when(j == pl.num_programs(1) - 1)
        def _():
            o_ref[...] = jnp.swapaxes(acc[...], 0, 1).reshape(tm, d)


def _ffn(x, gain, w_gu, w_down, layer, *, tm, tf, layout="rows", chunk=None):
    d = x.shape[-1]
    m = x.size // d
    f = w_down.shape[1]
    nf = f // tf
    rows_spec = pl.BlockSpec((tm, d), lambda i, j: (i, 0))
    rows_shape = jax.ShapeDtypeStruct((m, d), F32)
    scratch = [pltpu.VMEM((tm, d), BF16)]
    if layout == "rows":
        x_spec, o_spec, o_shape = rows_spec, rows_spec, rows_shape
    else:
        cpb = tm // chunk
        acc_shape = (chunk, cpb, d)
        chunk_spec = pl.BlockSpec(acc_shape, lambda i, j: (0, i, 0))
        chunk_shape = jax.ShapeDtypeStruct((chunk, m // chunk, d), F32)
        if layout == "to_chunked":
            x_spec, o_spec, o_shape = rows_spec, chunk_spec, chunk_shape
        else:
            x_spec, o_spec, o_shape = chunk_spec, rows_spec, rows_shape
            scratch.append(pltpu.VMEM(acc_shape, F32))
    return pl.pallas_call(
        functools.partial(_ffn_body, layout=layout),
        grid=(m // tm, nf),
        in_specs=[
            x_spec,
            pl.BlockSpec((None, 1, d), lambda i, j: (layer, 0, 0)),
            pl.BlockSpec((None, d, tf), lambda i, j: (layer, 0, j)),
            pl.BlockSpec((None, d, tf), lambda i, j: (layer, 0, nf + j)),
            pl.BlockSpec((None, tf, d), lambda i, j: (layer, j, 0)),
        ],
        out_specs=o_spec,
        out_shape=o_shape,
        scratch_shapes=scratch,
        compiler_params=_cparams("parallel", "arbitrary"),
        name="ffn_" + layout,
    )(x, gain, w_gu, w_gu, w_down)


def _sb_inproj_body(x_ref, gain_ref, w_ref, o_ref, h_ref, *, n_q_blocks, q_scale):
    j = pl.program_id(1)

    @pl.when(j == 0)
    def _():
        h_ref[...] = _rms(x_ref[...], gain_ref[...]).astype(BF16)

    acc = _dot(h_ref[...], w_ref[...])
    scale = jnp.where(j < n_q_blocks, q_scale, 1.0).astype(F32)
    o_ref[...] = (acc * scale).astype(o_ref.dtype)


def _sb_inproj(x, gain, w_in, layer, j_layer, *, tm, tn, tok_w):
    m, d = x.shape
    n = w_in.shape[2]
    body = functools.partial(_sb_inproj_body, n_q_blocks=tok_w // tn,
                             q_scale=math.log2(math.e) / math.sqrt(HEAD_DIM))
    return pl.pallas_call(
        body,
        grid=(m // tm, n // tn),
        in_specs=[
            pl.BlockSpec((tm, d), lambda i, j: (i, 0)),
            pl.BlockSpec((None, 1, d), lambda i, j: (layer, 0, 0)),
            pl.BlockSpec((None, d, tn), lambda i, j: (j_layer, 0, j)),
        ],
        out_specs=pl.BlockSpec((tm, tn), lambda i, j: (i, j)),
        out_shape=jax.ShapeDtypeStruct((m, n), BF16),
        scratch_shapes=[pltpu.VMEM((tm, d), BF16)],
        compiler_params=_cparams("parallel", "arbitrary"),
        name="sb_inproj",
    )(x, gain, w_in)


def _sb_attn_body(q_ref, k_ref, v_ref, o_ref, acc_ref, carry_ref, *, tq, tk, hps):
    qi = pl.program_id(2)
    diag_blocks = tq // tk
    row = lax.broadcasted_iota(jnp.int32, (tk, tk), 0)
    col = lax.broadcasted_iota(jnp.int32, (tk, tk), 1)
    suffix = (row > col).astype(BF16)
    q_pos = lax.broadcasted_iota(jnp.int32, (tq, tk), 0)
    k_pos = lax.broadcasted_iota(jnp.int32, (tq, tk), 1)

    def visit(kb, diag_index):
        start = pl.multiple_of(kb * tk, tk)
        heads = [slice(hd * HEAD_DIM, (hd + 1) * HEAD_DIM) for hd in range(hps)]
        if diag_index is not None:
            before = (k_pos + diag_index * tk) < q_pos
        zs, sps = [], []
        for sl in heads:
            z = _dot_nt(q_ref[:, sl], k_ref[pl.ds(start, tk), sl])
            neg_abs = lax.bitcast_convert_type(
                lax.bitcast_convert_type(z, jnp.uint32) | jnp.uint32(0x80000000), F32)
            sp = jnp.maximum(z, 0.0) + jnp.log2(1.0 + jnp.exp2(neg_abs))
            if diag_index is not None:
                sp = jnp.where(before, sp, 0.0)
            zs.append(z)
            sps.append(sp)
        spbs = [sp.astype(BF16) for sp in sps]
        inner_all = _dot(jnp.concatenate(spbs, axis=0), suffix)
        for hd, sl in enumerate(heads):
            inner = inner_all[hd * tq:(hd + 1) * tq]
            total = inner[:, 0:1] + spbs[hd][:, 0:1].astype(F32)
            carry = carry_ref[:, sl]
            carry_k = jnp.concatenate([carry] * (tk // HEAD_DIM), axis=1)
            w = jnp.exp2((zs[hd] - sps[hd]) - inner - carry_k)
            if diag_index is not None:
                w = jnp.where(before, w, 0.0)
            acc_ref[:, sl] += _dot(w.astype(BF16), v_ref[pl.ds(start, tk), sl])
            carry_ref[:, sl] = carry + jnp.broadcast_to(total, carry.shape)

    acc_ref[...] = jnp.zeros_like(acc_ref)
    carry_ref[...] = jnp.zeros_like(carry_ref)
    for j in reversed(range(diag_blocks)):
        visit(qi * diag_blocks + j, j)

    def step(i, _):
        visit(qi * diag_blocks - 1 - i, None)
        return 0

    lax.fori_loop(0, qi * diag_blocks, step, 0)
    o_ref[...] = acc_ref[...].astype(o_ref.dtype)


def _sb_attention(qkv, *, batch, seq, heads, tq, tk, hps):
    nq = seq // tq
    hg = heads // hps
    w = hps * HEAD_DIM
    body = functools.partial(_sb_attn_body, tq=tq, tk=tk, hps=hps)
    return pl.pallas_call(
        body,
        grid=(batch, hg, nq),
        in_specs=[
            pl.BlockSpec((tq, w), lambda b, h, i: (b * nq + i, h)),
            pl.BlockSpec((seq, w), lambda b, h, i: (b, hg + h)),
            pl.BlockSpec((seq, w), lambda b, h, i: (b, 2 * hg + h)),
        ],
        out_specs=pl.BlockSpec((tq, w), lambda b, h, i: (b * nq + i, h)),
        out_shape=jax.ShapeDtypeStruct((batch * seq, heads * HEAD_DIM), BF16),
        scratch_shapes=[pltpu.VMEM((tq, w), F32), pltpu.VMEM((tq, w), F32)],
        compiler_params=_cparams("parallel", "parallel", "arbitrary"),
        name="sb_attention",
    )(qkv, qkv, qkv)


def _mem_kv_body(mem_ref, gain_ref, w_ref, kgain_ref, k_ref, v_ref):
    h = _rms(mem_ref[...], gain_ref[...]).astype(BF16)
    kv = _dot(h, w_ref[...])
    for hd in range(MEM_HEADS):
        sl = slice(hd * HEAD_DIM, (hd + 1) * HEAD_DIM)
        k_ref[:, sl] = _rms(kv[:, sl], kgain_ref[...]).astype(BF16)
    v_ref[...] = kv[:, MEM_W:].astype(BF16)


def _mem_kv(mem, gain, w_kv, kgain, layer):
    rows, d = mem.shape
    return pl.pallas_call(
        _mem_kv_body,
        grid=(1,),
        in_specs=[
            pl.BlockSpec((rows, d), lambda i: (0, 0)),
            pl.BlockSpec((None, 1, d), lambda i: (layer, 0, 0)),
            pl.BlockSpec((None, d, 2 * MEM_W), lambda i: (layer, 0, 0)),
            pl.BlockSpec((None, 1, HEAD_DIM), lambda i: (layer, 0, 0)),
        ],
        out_specs=[pl.BlockSpec((rows, MEM_W), lambda i: (0, 0))] * 2,
        out_shape=[jax.ShapeDtypeStruct((rows, MEM_W), BF16)] * 2,
        compiler_params=_cparams("arbitrary"),
        name="mem_kv",
    )(mem, gain, w_kv, kgain)


def _cross_attention(qm, k, v, qgain):
    outs = []
    for hd in range(MEM_HEADS):
        sl = slice(hd * HEAD_DIM, (hd + 1) * HEAD_DIM)
        qn = _rms(qm[:, sl], qgain).astype(BF16)
        s = _dot_nt(qn, k[:, sl]) * (1.0 / math.sqrt(HEAD_DIM))
        p = jnp.exp(s - jnp.max(s, axis=-1, keepdims=True))
        p = p / jnp.sum(p, axis=-1, keepdims=True)
        outs.append(_dot(p.astype(BF16), v[:, sl]))
    return jnp.concatenate(outs, axis=1)


def _sb_outproj_body(x_ref, tok_ref, qm_ref, k_ref, v_ref, qgain_ref, w1_ref, w2_ref, o_ref):
    cross = _cross_attention(qm_ref[...].astype(F32), k_ref[...], v_ref[...], qgain_ref[...])
    o_ref[...] = (x_ref[...] + _dot(tok_ref[...], w1_ref[...])
                  + _dot(cross.astype(BF16), w2_ref[...]))


def _sb_outproj(x, tok, qkv, kmem, vmem, qgain, w_out, layer, *, tm, seq, tok_w):
    m, d = x.shape
    n_mem = kmem.shape[0] // (m // seq)
    per_batch = seq // tm
    qm_block = tok_w * 3 // MEM_W
    return pl.pallas_call(
        _sb_outproj_body,
        grid=(m // tm,),
        in_specs=[
            pl.BlockSpec((tm, d), lambda i: (i, 0)),
            pl.BlockSpec((tm, tok_w), lambda i: (i, 0)),
            pl.BlockSpec((tm, MEM_W), lambda i: (i, qm_block)),
            pl.BlockSpec((n_mem, MEM_W), lambda i: (i // per_batch, 0)),
            pl.BlockSpec((n_mem, MEM_W), lambda i: (i // per_batch, 0)),
            pl.BlockSpec((None, 1, HEAD_DIM), lambda i: (layer, 0, 0)),
            pl.BlockSpec((None, tok_w, d), lambda i: (layer, 0, 0)),
            pl.BlockSpec((None, MEM_W, d), lambda i: (layer, tok_w // MEM_W, 0)),
        ],
        out_specs=pl.BlockSpec((tm, d), lambda i: (i, 0)),
        out_shape=jax.ShapeDtypeStruct((m, d), F32),
        compiler_params=_cparams("parallel"),
        name="sb_outproj",
    )(x, tok, qkv, kmem, vmem, qgain, w_out, w_out)


def _s5_inproj_body(x_ref, gain_ref, wut_ref, wq_ref, u_ref, qm_ref):
    h = _rms(x_ref[...], gain_ref[...]).astype(BF16)
    ut = _dot_nt(wut_ref[...], h)
    u_ref[...] = ut.reshape(u_ref.shape)
    qm_ref[0] = _dot(h, wq_ref[...]).astype(qm_ref.dtype)


def _s5_inproj(xc, gain, wut, wq, layer, j_layer):
    t, n, d = xc.shape
    tok_w = wut.shape[1]
    groups = tok_w // S5_GROUP
    return pl.pallas_call(
        _s5_inproj_body,
        grid=(t,),
        in_specs=[
            pl.BlockSpec((None, n, d), lambda i: (i, 0, 0)),
            pl.BlockSpec((None, 1, d), lambda i: (layer, 0, 0)),
            pl.BlockSpec((None, tok_w, d), lambda i: (j_layer, 0, 0)),
            pl.BlockSpec((None, d, MEM_W), lambda i: (j_layer, 0, 0)),
        ],
        out_specs=[
            pl.BlockSpec((groups, 1, S5_GROUP, n), lambda i: (0, i, 0, 0)),
            pl.BlockSpec((1, n, MEM_W), lambda i: (i, 0, 0)),
        ],
        out_shape=[
            jax.ShapeDtypeStruct((groups, t, S5_GROUP, n), F32),
            jax.ShapeDtypeStruct((t, n, MEM_W), BF16),
        ],
        compiler_params=_cparams("parallel"),
        name="s5_inproj",
    )(xc, gain, wut, wq)


def _cexp(zr, zi, k):
    mag = jnp.exp(zr * k)
    return mag * jnp.cos(zi * k), mag * jnp.sin(zi * k)


def _s5_core_body(u_ref, ldt_ref, arc_ref, aic_ref, br_ref, bi_ref,
                  cr_ref, ci_ref, d_ref, exp_ref, tile_ref, y_ref, mt_ref, q_ref, *, n_chunks):
    t = S5_CHUNK
    c = S5_GROUP
    ns = S5_STATE
    n = u_ref.shape[-1]
    dt = jnp.exp(ldt_ref[0])

    lr, li = arc_ref[0], aic_ref[0]
    zr, zi = lr * dt, li * dt
    back = (t - 1 - lax.broadcasted_iota(jnp.int32, (ns, t), 1)).astype(F32)
    pwr, pwi = _cexp(zr, zi, back)
    abr, abi = pwr[:, t - 2:t - 1], pwi[:, t - 2:t - 1]
    top_r, top_i = pwr[:, 0:1], pwi[:, 0:1]
    mr, mi = top_r * abr - top_i * abi, top_r * abi + top_i * abr
    pwr_t, pwi_t = pwr.T, pwi.T

    nr, ni = abr - 1.0, abi
    den = lr * lr + li * li
    fr = (nr * lr + ni * li) / den
    fi = (ni * lr - nr * li) / den
    b_re, b_im = br_ref[0], bi_ref[0]
    bbr = fr * b_re - fi * b_im
    bbi = fr * b_im + fi * b_re

    pwr_x, pwi_x = _dot_sel(pwr, exp_ref[...]), _dot_sel(pwi, exp_ref[...])
    bbr_x, bbi_x = _dot_sel(bbr, tile_ref[...]), _dot_sel(bbi, tile_ref[...])
    p_re = pwr_x * bbr_x - pwi_x * bbi_x
    p_im = pwr_x * bbi_x + pwi_x * bbr_x
    p = jnp.concatenate([p_re, p_im], axis=0)

    c_re, c_im = cr_ref[0], ci_ref[0]
    c_ext = jnp.concatenate([c_re, -c_im], axis=1)
    k_rev = jnp.concatenate([_dot3(c_ext, p), jnp.zeros((c, t * c), F32)], axis=1)
    width = 2 * t * c
    for step in range(t):
        off = (t - 1 - step) * c
        win = k_rev if off == 0 else pltpu.roll(k_rev, width - off, 1)
        mt_ref[step * c:(step + 1) * c, :] = win[:, :t * c].astype(BF16)

    for step in range(t):
        if step < t - 1:
            src = t - 2 - step
            ar, ai = pwr_t[src:src + 1, :], pwi_t[src:src + 1, :]
        else:
            hr, hi, lr_, li_ = pwr_t[0:1], pwi_t[0:1], pwr_t[t - 2:t - 1], pwi_t[t - 2:t - 1]
            ar, ai = hr * lr_ - hi * li_, hr * li_ + hi * lr_
        q_ref[step * c:(step + 1) * c, 0:ns] = (c_re * ar - c_im * ai).astype(BF16)
        q_ref[step * c:(step + 1) * c, ns:2 * ns] = (-(c_re * ai + c_im * ar)).astype(BF16)

    u = u_ref[0].reshape(t * c, n)
    ub = u.astype(BF16)
    inj = _dot(p.astype(BF16), ub)
    s_re, s_im = inj[:ns], inj[ns:]

    pos = lax.broadcasted_iota(jnp.int32, (ns, n), 1) % n_chunks
    shift = 1
    while shift < n_chunks:
        keep = pos >= shift
        sh_re = jnp.where(keep, pltpu.roll(s_re, shift, 1), 0.0)
        sh_im = jnp.where(keep, pltpu.roll(s_im, shift, 1), 0.0)
        s_re, s_im = s_re + mr * sh_re - mi * sh_im, s_im + mr * sh_im + mi * sh_re
        mr, mi = mr * mr - mi * mi, 2.0 * mr * mi
        shift *= 2
    first = pos >= 1
    prev = jnp.concatenate([jnp.where(first, pltpu.roll(s_re, 1, 1), 0.0),
                            jnp.where(first, pltpu.roll(s_im, 1, 1), 0.0)], axis=0)

    dcol = jnp.concatenate([d_ref[0]] * t, axis=0)
    y = _dot(mt_ref[...], ub) + _dot(q_ref[...], prev.astype(BF16)) + dcol * u
    y_ref[0] = jax.nn.gelu(y).reshape(t, c, n)


def _s5_core(u4, ldt, arc, aic, b_re, b_im, c_re, c_im, dcol, expand, tile, j_layer, *, n_chunks):
    groups, t, c, n = u4.shape
    ns = S5_STATE
    body = functools.partial(_s5_core_body, n_chunks=n_chunks)

    def per_group(shape):
        return pl.BlockSpec((None, 1) + shape, lambda g: (j_layer, g, 0, 0))

    return pl.pallas_call(
        body,
        grid=(groups,),
        in_specs=[
            pl.BlockSpec((1, t, c, n), lambda g: (g, 0, 0, 0)),
            per_group((1, 1)),
            per_group((ns, 1)), per_group((ns, 1)),
            per_group((ns, c)), per_group((ns, c)),
            per_group((c, ns)), per_group((c, ns)),
            per_group((c, 1)),
            pl.BlockSpec((t, t * c), lambda g: (0, 0)),
            pl.BlockSpec((c, t * c), lambda g: (0, 0)),
        ],
        out_specs=pl.BlockSpec((1, t, c, n), lambda g: (g, 0, 0, 0)),
        out_shape=jax.ShapeDtypeStruct((groups, t, c, n), F32),
        scratch_shapes=[pltpu.VMEM((t * c, t * c), BF16), pltpu.VMEM((t * c, 2 * ns), BF16)],
        compiler_params=_cparams("parallel"),
        name="s5_core",
    )(u4, ldt, arc, aic, b_re, b_im, c_re, c_im, dcol, expand, tile)


def _s5_outproj_body(x_ref, y_ref, qm_ref, k_ref, v_ref, qgain_ref, wglut_ref, woutt_ref, o_ref,
                     *, n_mem, nc):
    steps, n, d = x_ref.shape
    b = n // nc
    groups, _, c, _ = y_ref.shape
    for s in range(steps):
        yt = y_ref[:, s].reshape(groups * c, n)
        gate = _dot(wglut_ref[...], yt.astype(BF16))
        tok_t = (yt * jax.nn.sigmoid(gate)).astype(BF16)
        qm = qm_ref[s].astype(F32)
        cross = jnp.concatenate(
            [_cross_attention(qm[i * nc:(i + 1) * nc], k_ref[i * n_mem:(i + 1) * n_mem],
                              v_ref[i * n_mem:(i + 1) * n_mem], qgain_ref[...])
             for i in range(b)], axis=0)
        mixed_t = jnp.concatenate([tok_t, cross.T.astype(BF16)], axis=0)
        upd = _dot(woutt_ref[...], mixed_t).T
        o_ref[s] = x_ref[s] + upd


def _s5_outproj(xc, y4, qm, kmem, vmem, qgain, wglut, woutt, layer, j_layer, *, n_chunks, steps):
    t, n, d = xc.shape
    groups, _, c, _ = y4.shape
    tok_w = groups * c
    b = n // n_chunks
    n_mem = kmem.shape[0] // b
    body = functools.partial(_s5_outproj_body, n_mem=n_mem, nc=n_chunks)
    return pl.pallas_call(
        body,
        grid=(t // steps,),
        in_specs=[
            pl.BlockSpec((steps, n, d), lambda i: (i, 0, 0)),
            pl.BlockSpec((groups, steps, c, n), lambda i: (0, i, 0, 0)),
            pl.BlockSpec((steps, n, MEM_W), lambda i: (i, 0, 0)),
            pl.BlockSpec((b * n_mem, MEM_W), lambda i: (0, 0)),
            pl.BlockSpec((b * n_mem, MEM_W), lambda i: (0, 0)),
            pl.BlockSpec((None, 1, HEAD_DIM), lambda i: (layer, 0, 0)),
            pl.BlockSpec((None, tok_w, tok_w), lambda i: (j_layer, 0, 0)),
            pl.BlockSpec((None, d, d), lambda i: (layer, 0, 0)),
        ],
        out_specs=pl.BlockSpec((steps, n, d), lambda i: (i, 0, 0)),
        out_shape=jax.ShapeDtypeStruct(xc.shape, F32),
        compiler_params=_cparams("parallel"),
        name="s5_outproj",
    )(xc, y4, qm, kmem, vmem, qgain, wglut, woutt)


FFN_TM = 512
FFN_TF = 512
PROJ_TM = 1024
PROJ_TN = 512
OUT_TM = 512
S5_OUT_STEPS = 2
SB_TQ = 512
SB_TK = 256
SB_HEADS_PER_STEP = 4


def kernel(x, mem, ffn1_norm, ffn1_w_gu, ffn1_w_down, mix_norm, mem_norm, w_mem_kv, xq_norm,
           xk_norm, w_out, ffn2_norm, ffn2_w_gu, ffn2_w_down, sb_w_in, s5_w_in, s5_log_dt,
           s5_a_re, s5_a_im, s5_b_re, s5_b_im, s5_c_re, s5_c_im, s5_d, s5_w_glu):
    batch, seq, d = x.shape
    depth = ffn1_norm.shape[0]
    n_mem = mem.shape[1]
    tok_w = d - MEM_W
    heads = tok_w // HEAD_DIM
    groups = tok_w // S5_GROUP
    n_chunks = seq // S5_CHUNK
    n_b = s5_w_in.shape[0]
    ns, c, t = S5_STATE, S5_GROUP, S5_CHUNK

    def row(g):
        return g.reshape(g.shape[0], 1, g.shape[1])

    ffn1_norm, ffn2_norm, mix_norm, mem_norm = map(row, (ffn1_norm, ffn2_norm, mix_norm, mem_norm))
    xq_norm, xk_norm = row(xq_norm), row(xk_norm)
    w1_gu, w1_down = ffn1_w_gu.astype(BF16), ffn1_w_down.astype(BF16)
    w2_gu, w2_down = ffn2_w_gu.astype(BF16), ffn2_w_down.astype(BF16)
    w_kv = w_mem_kv.astype(BF16)
    w_out_b = w_out.astype(BF16)
    w_out_t = jnp.swapaxes(w_out, 1, 2).astype(BF16)
    sb_w = sb_w_in.astype(BF16)
    s5_wu_t = jnp.swapaxes(s5_w_in[:, :, :tok_w], 1, 2).astype(BF16)
    s5_wq = s5_w_in[:, :, tok_w:].astype(BF16)
    s5_wglu_t = jnp.swapaxes(s5_w_glu, 1, 2).astype(BF16)

    ldt = s5_log_dt.reshape(n_b, groups, 1, 1)
    arc, aic = s5_a_re.reshape(n_b, groups, ns, 1), s5_a_im.reshape(n_b, groups, ns, 1)
    dcol = s5_d.reshape(n_b, groups, c, 1)
    lane = jnp.arange(t * c)
    expand = (lane[None, :] // c == jnp.arange(t)[:, None]).astype(BF16)
    tile = (lane[None, :] % c == jnp.arange(c)[:, None]).astype(BF16)

    mem2 = mem.reshape(batch * n_mem, d)
    xs = x.reshape(batch * seq, d)
    ffn = functools.partial(_ffn, tm=FFN_TM, tf=FFN_TF, chunk=t)
    for layer in range(depth):
        kmem, vmem = _mem_kv(mem2, mem_norm, w_kv, xk_norm, layer)
        j_layer = layer // 2
        if layer % 2 == 0:
            xs = ffn(xs, ffn1_norm, w1_gu, w1_down, layer)
            qkv = _sb_inproj(xs, mix_norm, sb_w, layer, j_layer, tm=PROJ_TM, tn=PROJ_TN,
                             tok_w=tok_w)
            tok = _sb_attention(qkv, batch=batch, seq=seq, heads=heads, tq=SB_TQ, tk=SB_TK,
                                hps=SB_HEADS_PER_STEP)
            xs = _sb_outproj(xs, tok, qkv, kmem, vmem, xq_norm, w_out_b, layer, tm=OUT_TM,
                             seq=seq, tok_w=tok_w)
            xs = ffn(xs, ffn2_norm, w2_gu, w2_down, layer)
        else:
            xc = ffn(xs, ffn1_norm, w1_gu, w1_down, layer, layout="to_chunked")
            u4, qm = _s5_inproj(xc, mix_norm, s5_wu_t, s5_wq, layer, j_layer)
            y4 = _s5_core(u4, ldt, arc, aic, s5_b_re, s5_b_im, s5_c_re, s5_c_im, dcol,
                          expand, tile, j_layer, n_chunks=n_chunks)
            xc = _s5_outproj(xc, y4, qm, kmem, vmem, xq_norm, s5_wglu_t, w_out_t, layer, j_layer,
                             n_chunks=n_chunks, steps=S5_OUT_STEPS)
            xs = ffn(xc, ffn2_norm, w2_gu, w2_down, layer, layout="from_chunked")
    return xs.reshape(batch, seq, d)
```

```python
import functools
import math

import jax
import jax.numpy as jnp
from jax import lax
from jax.experimental import pallas as pl
from jax.experimental.pallas import tpu as pltpu

F32 = jnp.float32
BF16 = jnp.bfloat16

EPS = 1e-6
HEAD_DIM = 128
MEM_HEADS = 4
MEM_W = MEM_HEADS * HEAD_DIM
S5_GROUP = 16
S5_STATE = 64
S5_CHUNK = 64
F32_EXP2_UNDERFLOW = 151.0

V7X_VMEM_LIMIT_BYTES = 56 * 1024 * 1024


def _cparams(*sem):
    return pltpu.CompilerParams(dimension_semantics=sem, vmem_limit_bytes=V7X_VMEM_LIMIT_BYTES)


def _rms(x, gain):
    return x * lax.rsqrt(jnp.mean(x * x, axis=-1, keepdims=True) + EPS) * gain


def _dot(a, b):
    return jnp.dot(a, b, preferred_element_type=F32)


def _dot_nt(a, b):
    return lax.dot_general(a, b, (((1,), (1,)), ((), ())), preferred_element_type=F32)


def _split3(x):
    h = x.astype(BF16)
    r = x - h.astype(F32)
    m = r.astype(BF16)
    l = (r - m.astype(F32)).astype(BF16)
    return h, m, l


def _dot_sel(x, sel):
    h, m, l = _split3(x)
    return _dot(h, sel) + _dot(m, sel) + _dot(l, sel)


def _dot3(a, b):
    ah = a.astype(BF16)
    al = (a - ah.astype(F32)).astype(BF16)
    bh = b.astype(BF16)
    bl = (b - bh.astype(F32)).astype(BF16)
    return _dot(ah, bh) + _dot(ah, bl) + _dot(al, bh)


def _ffn_body(*refs, layout, cast_next):
    n_extra = 2 if cast_next else 0
    x_ref, gain_ref, wg_ref, wu_ref, wd_ref = refs[:5]
    next_in = refs[5:5 + n_extra]
    o_ref = refs[5 + n_extra]
    next_out = refs[6 + n_extra:6 + 2 * n_extra]
    h_ref, *maybe_acc_ref = refs[6 + 2 * n_extra:]

    j = pl.program_id(1)
    tm, d = h_ref.shape
    acc = maybe_acc_ref[0] if layout == "from_chunked" else o_ref

    @pl.when(j == 0)
    def _():
        x = x_ref[...]
        if layout == "to_chunked":
            t, cpb, _ = acc.shape
            x = jnp.swapaxes(x.reshape(cpb, t, d), 0, 1)
        acc[...] = x.reshape(acc.shape)
        h_ref[...] = _rms(x.reshape(tm, d), gain_ref[...]).astype(BF16)

    h = h_ref[...]
    g = _dot(h, wg_ref[...])
    u = _dot(h, wu_ref[...])
    a = (g * jax.nn.sigmoid(g) * (u * 0.5)).astype(BF16)
    acc[...] += _dot(a, wd_ref[...]).reshape(acc.shape)

    for src, dst in zip(next_in, next_out):
        dst[...] = src[...].astype(BF16)

    if layout == "from_chunked":
        @pl.when(j == pl.num_programs(1) - 1)
        def _():
            o_ref[...] = jnp.swapaxes(acc[...], 0, 1).reshape(tm, d)


def _ffn(x, gain, w_gu, w_down, layer, *, tm, tf, layout="rows", chunk=None, cast_next=None):
    d = x.shape[-1]
    m = x.size // d
    f = w_down.shape[0]
    nf = f // tf
    n_i = m // tm
    rows_spec = pl.BlockSpec((tm, d), lambda i, j: (i, 0))
    rows_shape = jax.ShapeDtypeStruct((m, d), F32)
    scratch = [pltpu.VMEM((tm, d), BF16)]
    if layout == "rows":
        x_spec, o_spec, o_shape = rows_spec, rows_spec, rows_shape
    else:
        cpb = tm // chunk
        acc_shape = (chunk, cpb, d)
        chunk_spec = pl.BlockSpec(acc_shape, lambda i, j: (0, i, 0))
        chunk_shape = jax.ShapeDtypeStruct((chunk, m // chunk, d), F32)
        if layout == "to_chunked":
            x_spec, o_spec, o_shape = rows_spec, chunk_spec, chunk_shape
        else:
            x_spec, o_spec, o_shape = chunk_spec, rows_spec, rows_shape
            scratch.append(pltpu.VMEM(acc_shape, F32))
    in_specs = [
        x_spec,
        pl.BlockSpec((None, 1, d), lambda i, j: (layer, 0, 0)),
        pl.BlockSpec((d, tf), lambda i, j: (0, j)),
        pl.BlockSpec((d, tf), lambda i, j: (0, nf + j)),
        pl.BlockSpec((tf, d), lambda i, j: (j, 0)),
    ]
    args = [x, gain, w_gu, w_gu, w_down]
    out_specs, out_shapes = [o_spec], [o_shape]
    if cast_next is not None:
        gu_f32, down_f32, nxt = cast_next
        gu_rows, gu_cols, down_rows = d // n_i, 2 * f // nf, f // (n_i * nf)
        assert gu_rows * n_i == d and gu_cols * nf == 2 * f and down_rows * n_i * nf == f
        in_specs += [pl.BlockSpec((None, gu_rows, gu_cols), lambda i, j: (nxt, i, j)),
                     pl.BlockSpec((None, down_rows, d), lambda i, j: (nxt, i * nf + j, 0))]
        args += [gu_f32, down_f32]
        out_specs += [pl.BlockSpec((gu_rows, gu_cols), lambda i, j: (i, j)),
                      pl.BlockSpec((down_rows, d), lambda i, j: (i * nf + j, 0))]
        out_shapes += [jax.ShapeDtypeStruct((d, 2 * f), BF16), jax.ShapeDtypeStruct((f, d), BF16)]
    outs = pl.pallas_call(
        functools.partial(_ffn_body, layout=layout, cast_next=cast_next is not None),
        grid=(n_i, nf),
        in_specs=in_specs,
        out_specs=out_specs,
        out_shape=out_shapes,
        scratch_shapes=scratch,
        compiler_params=_cparams("parallel", "arbitrary"),
        name="ffn_" + layout,
    )(*args)
    return outs[0], tuple(outs[1:])


def _sb_inproj_body(x_ref, gain_ref, w_ref, o_ref, h_ref, *, n_q_blocks, q_scale):
    j = pl.program_id(1)

    @pl.when(j == 0)
    def _():
        h_ref[...] = _rms(x_ref[...], gain_ref[...]).astype(BF16)

    acc = _dot(h_ref[...], w_ref[...])
    scale = jnp.where(j < n_q_blocks, q_scale, 1.0).astype(F32)
    o_ref[...] = (acc * scale).astype(o_ref.dtype)


def _sb_inproj(x, gain, w_in, layer, j_layer, *, tm, tn, tok_w):
    m, d = x.shape
    n = w_in.shape[2]
    body = functools.partial(_sb_inproj_body, n_q_blocks=tok_w // tn,
                             q_scale=math.log2(math.e) / math.sqrt(HEAD_DIM))
    return pl.pallas_call(
        body,
        grid=(m // tm, n // tn),
        in_specs=[
            pl.BlockSpec((tm, d), lambda i, j: (i, 0)),
            pl.BlockSpec((None, 1, d), lambda i, j: (layer, 0, 0)),
            pl.BlockSpec((None, d, tn), lambda i, j: (j_layer, 0, j)),
        ],
        out_specs=pl.BlockSpec((tm, tn), lambda i, j: (i, j)),
        out_shape=jax.ShapeDtypeStruct((m, n), BF16),
        scratch_shapes=[pltpu.VMEM((tm, d), BF16)],
        compiler_params=_cparams("parallel", "arbitrary"),
        name="sb_inproj",
    )(x, gain, w_in)


def _sb_attn_body(q_ref, k_ref, v_ref, o_ref, acc_ref, carry_ref, *, tq, tk, hps):
    qi = pl.program_id(2)
    diag_blocks = tq // tk
    row = lax.broadcasted_iota(jnp.int32, (tk, tk), 0)
    col = lax.broadcasted_iota(jnp.int32, (tk, tk), 1)
    suffix = (row > col).astype(BF16)
    q_pos = lax.broadcasted_iota(jnp.int32, (tq, tk), 0)
    k_pos = lax.broadcasted_iota(jnp.int32, (tq, tk), 1)

    def visit(kb, diag_index):
        start = pl.multiple_of(kb * tk, tk)
        heads = [slice(hd * HEAD_DIM, (hd + 1) * HEAD_DIM) for hd in range(hps)]
        if diag_index is not None:
            before = (k_pos + diag_index * tk) < q_pos
        zs, sps = [], []
        for sl in heads:
            z = _dot_nt(q_ref[:, sl], k_ref[pl.ds(start, tk), sl])
            neg_abs = lax.bitcast_convert_type(
                lax.bitcast_convert_type(z, jnp.uint32) | jnp.uint32(0x80000000), F32)
            sp = jnp.maximum(z, 0.0) + jnp.log2(1.0 + jnp.exp2(neg_abs))
            if diag_index is not None:
                sp = jnp.where(before, sp, 0.0)
            zs.append(z)
            sps.append(sp)
        spbs = [sp.astype(BF16) for sp in sps]
        inner_all = _dot(jnp.concatenate(spbs, axis=0), suffix)
        for hd, sl in enumerate(heads):
            inner = inner_all[hd * tq:(hd + 1) * tq]
            total = inner[:, 0:1] + spbs[hd][:, 0:1].astype(F32)
            carry = carry_ref[:, sl]
            carry_k = jnp.concatenate([carry] * (tk // HEAD_DIM), axis=1)
            w = jnp.exp2((zs[hd] - sps[hd]) - inner - carry_k)
            if diag_index is not None:
                w = jnp.where(before, w, 0.0)
            acc_ref[:, sl] += _dot(w.astype(BF16), v_ref[pl.ds(start, tk), sl])
            carry_ref[:, sl] = carry + jnp.broadcast_to(total, carry.shape)

    acc_ref[...] = jnp.zeros_like(acc_ref)
    carry_ref[...] = jnp.zeros_like(carry_ref)
    for j in reversed(range(diag_blocks)):
        visit(qi * diag_blocks + j, j)

    n_far = qi * diag_blocks

    def more(state):
        i, least_carry = state
        return jnp.logical_and(i < n_far, least_carry < F32_EXP2_UNDERFLOW)

    def step(state):
        i, _ = state
        visit(n_far - 1 - i, None)
        return i + 1, jnp.min(carry_ref[...])

    lax.while_loop(more, step, (jnp.int32(0), jnp.min(carry_ref[...])))
    o_ref[...] = acc_ref[...].astype(o_ref.dtype)


def _sb_attention(qkv, *, batch, seq, heads, tq, tk, hps):
    nq = seq // tq
    hg = heads // hps
    w = hps * HEAD_DIM
    body = functools.partial(_sb_attn_body, tq=tq, tk=tk, hps=hps)
    return pl.pallas_call(
        body,
        grid=(batch, hg, nq),
        in_specs=[
            pl.BlockSpec((tq, w), lambda b, h, i: (b * nq + i, h)),
            pl.BlockSpec((seq, w), lambda b, h, i: (b, hg + h)),
            pl.BlockSpec((seq, w), lambda b, h, i: (b, 2 * hg + h)),
        ],
        out_specs=pl.BlockSpec((tq, w), lambda b, h, i: (b * nq + i, h)),
        out_shape=jax.ShapeDtypeStruct((batch * seq, heads * HEAD_DIM), BF16),
        scratch_shapes=[pltpu.VMEM((tq, w), F32), pltpu.VMEM((tq, w), F32)],
        compiler_params=_cparams("parallel", "parallel", "arbitrary"),
        name="sb_attention",
    )(qkv, qkv, qkv)


def _mem_kv_body(mem_ref, gain_ref, w_ref, kgain_ref, k_ref, v_ref):
    h = _rms(mem_ref[...], gain_ref[...]).astype(BF16)
    kv = _dot(h, w_ref[...])
    for hd in range(MEM_HEADS):
        sl = slice(hd * HEAD_DIM, (hd + 1) * HEAD_DIM)
        k_ref[:, sl] = _rms(kv[:, sl], kgain_ref[...]).astype(BF16)
    v_ref[...] = kv[:, MEM_W:].astype(BF16)


def _mem_kv(mem, gain, w_kv, kgain, layer):
    rows, d = mem.shape
    return pl.pallas_call(
        _mem_kv_body,
        grid=(1,),
        in_specs=[
            pl.BlockSpec((rows, d), lambda i: (0, 0)),
            pl.BlockSpec((None, 1, d), lambda i: (layer, 0, 0)),
            pl.BlockSpec((None, d, 2 * MEM_W), lambda i: (layer, 0, 0)),
            pl.BlockSpec((None, 1, HEAD_DIM), lambda i: (layer, 0, 0)),
        ],
        out_specs=[pl.BlockSpec((rows, MEM_W), lambda i: (0, 0))] * 2,
        out_shape=[jax.ShapeDtypeStruct((rows, MEM_W), BF16)] * 2,
        compiler_params=_cparams("arbitrary"),
        name="mem_kv",
    )(mem, gain, w_kv, kgain)


def _cross_attention(qm, k, v, qgain):
    outs = []
    for hd in range(MEM_HEADS):
        sl = slice(hd * HEAD_DIM, (hd + 1) * HEAD_DIM)
        qn = _rms(qm[:, sl], qgain).astype(BF16)
        s = _dot_nt(qn, k[:, sl]) * (1.0 / math.sqrt(HEAD_DIM))
        p = jnp.exp(s - jnp.max(s, axis=-1, keepdims=True))
        p = p / jnp.sum(p, axis=-1, keepdims=True)
        outs.append(_dot(p.astype(BF16), v[:, sl]))
    return jnp.concatenate(outs, axis=1)


def _sb_outproj_body(x_ref, tok_ref, qm_ref, k_ref, v_ref, qgain_ref, w1_ref, w2_ref, o_ref):
    cross = _cross_attention(qm_ref[...].astype(F32), k_ref[...], v_ref[...], qgain_ref[...])
    o_ref[...] = (x_ref[...] + _dot(tok_ref[...], w1_ref[...])
                  + _dot(cross.astype(BF16), w2_ref[...]))


def _sb_outproj(x, tok, qkv, kmem, vmem, qgain, w_out, layer, *, tm, seq, tok_w):
    m, d = x.shape
    n_mem = kmem.shape[0] // (m // seq)
    per_batch = seq // tm
    qm_block = tok_w * 3 // MEM_W
    return pl.pallas_call(
        _sb_outproj_body,
        grid=(m // tm,),
        in_specs=[
            pl.BlockSpec((tm, d), lambda i: (i, 0)),
            pl.BlockSpec((tm, tok_w), lambda i: (i, 0)),
            pl.BlockSpec((tm, MEM_W), lambda i: (i, qm_block)),
            pl.BlockSpec((n_mem, MEM_W), lambda i: (i // per_batch, 0)),
            pl.BlockSpec((n_mem, MEM_W), lambda i: (i // per_batch, 0)),
            pl.BlockSpec((None, 1, HEAD_DIM), lambda i: (layer, 0, 0)),
            pl.BlockSpec((None, tok_w, d), lambda i: (layer, 0, 0)),
            pl.BlockSpec((None, MEM_W, d), lambda i: (layer, tok_w // MEM_W, 0)),
        ],
        out_specs=pl.BlockSpec((tm, d), lambda i: (i, 0)),
        out_shape=jax.ShapeDtypeStruct((m, d), F32),
        compiler_params=_cparams("parallel"),
        name="sb_outproj",
    )(x, tok, qkv, kmem, vmem, qgain, w_out, w_out)


def _s5_inproj_body(x_ref, gain_ref, wut_ref, wq_ref, u_ref, qm_ref):
    h = _rms(x_ref[...], gain_ref[...]).astype(BF16)
    ut = _dot_nt(wut_ref[...], h)
    u_ref[...] = ut.reshape(u_ref.shape)
    qm_ref[0] = _dot(h, wq_ref[...]).astype(qm_ref.dtype)


def _s5_inproj(xc, gain, wut, wq, layer, j_layer):
    t, n, d = xc.shape
    tok_w = wut.shape[1]
    groups = tok_w // S5_GROUP
    return pl.pallas_call(
        _s5_inproj_body,
        grid=(t,),
        in_specs=[
            pl.BlockSpec((None, n, d), lambda i: (i, 0, 0)),
            pl.BlockSpec((None, 1, d), lambda i: (layer, 0, 0)),
            pl.BlockSpec((None, tok_w, d), lambda i: (j_layer, 0, 0)),
            pl.BlockSpec((None, d, MEM_W), lambda i: (j_layer, 0, 0)),
        ],
        out_specs=[
            pl.BlockSpec((groups, 1, S5_GROUP, n), lambda i: (0, i, 0, 0)),
            pl.BlockSpec((1, n, MEM_W), lambda i: (i, 0, 0)),
        ],
        out_shape=[
            jax.ShapeDtypeStruct((groups, t, S5_GROUP, n), F32),
            jax.ShapeDtypeStruct((t, n, MEM_W), BF16),
        ],
        compiler_params=_cparams("parallel"),
        name="s5_inproj",
    )(xc, gain, wut, wq)


def _cexp(zr, zi, k):
    mag = jnp.exp(zr * k)
    return mag * jnp.cos(zi * k), mag * jnp.sin(zi * k)


def _s5_core_body(u_ref, ldt_ref, arc_ref, aic_ref, br_ref, bi_ref,
                  cr_ref, ci_ref, d_ref, exp_ref, tile_ref, y_ref, mt_ref, q_ref, *, n_chunks):
    t = S5_CHUNK
    c = S5_GROUP
    ns = S5_STATE
    n = u_ref.shape[-1]
    dt = jnp.exp(ldt_ref[0])

    lr, li = arc_ref[0], aic_ref[0]
    zr, zi = lr * dt, li * dt
    back = (t - 1 - lax.broadcasted_iota(jnp.int32, (ns, t), 1)).astype(F32)
    pwr, pwi = _cexp(zr, zi, back)
    abr, abi = pwr[:, t - 2:t - 1], pwi[:, t - 2:t - 1]
    top_r, top_i = pwr[:, 0:1], pwi[:, 0:1]
    mr, mi = top_r * abr - top_i * abi, top_r * abi + top_i * abr
    pwr_t, pwi_t = pwr.T, pwi.T

    nr, ni = abr - 1.0, abi
    den = lr * lr + li * li
    fr = (nr * lr + ni * li) / den
    fi = (ni * lr - nr * li) / den
    b_re, b_im = br_ref[0], bi_ref[0]
    bbr = fr * b_re - fi * b_im
    bbi = fr * b_im + fi * b_re

    pwr_x, pwi_x = _dot_sel(pwr, exp_ref[...]), _dot_sel(pwi, exp_ref[...])
    bbr_x, bbi_x = _dot_sel(bbr, tile_ref[...]), _dot_sel(bbi, tile_ref[...])
    p_re = pwr_x * bbr_x - pwi_x * bbi_x
    p_im = pwr_x * bbi_x + pwi_x * bbr_x
    p = jnp.concatenate([p_re, p_im], axis=0)

    c_re, c_im = cr_ref[0], ci_ref[0]
    c_ext = jnp.concatenate([c_re, -c_im], axis=1)
    k_rev = jnp.concatenate([_dot3(c_ext, p), jnp.zeros((c, t * c), F32)], axis=1)
    width = 2 * t * c
    for step in range(t):
        off = (t - 1 - step) * c
        win = k_rev if off == 0 else pltpu.roll(k_rev, width - off, 1)
        mt_ref[step * c:(step + 1) * c, :] = win[:, :t * c].astype(BF16)

    for step in range(t):
        if step < t - 1:
            src = t - 2 - step
            ar, ai = pwr_t[src:src + 1, :], pwi_t[src:src + 1, :]
        else:
            hr, hi, lr_, li_ = pwr_t[0:1], pwi_t[0:1], pwr_t[t - 2:t - 1], pwi_t[t - 2:t - 1]
            ar, ai = hr * lr_ - hi * li_, hr * li_ + hi * lr_
        q_ref[step * c:(step + 1) * c, 0:ns] = (c_re * ar - c_im * ai).astype(BF16)
        q_ref[step * c:(step + 1) * c, ns:2 * ns] = (-(c_re * ai + c_im * ar)).astype(BF16)

    u = u_ref[0].reshape(t * c, n)
    ub = u.astype(BF16)
    inj = _dot(p.astype(BF16), ub)
    s_re, s_im = inj[:ns], inj[ns:]

    pos = lax.broadcasted_iota(jnp.int32, (ns, n), 1) % n_chunks
    shift = 1
    while shift < n_chunks:
        keep = pos >= shift
        sh_re = jnp.where(keep, pltpu.roll(s_re, shift, 1), 0.0)
        sh_im = jnp.where(keep, pltpu.roll(s_im, shift, 1), 0.0)
        s_re, s_im = s_re + mr * sh_re - mi * sh_im, s_im + mr * sh_im + mi * sh_re
        mr, mi = mr * mr - mi * mi, 2.0 * mr * mi
        shift *= 2
    first = pos >= 1
    prev = jnp.concatenate([jnp.where(first, pltpu.roll(s_re, 1, 1), 0.0),
                            jnp.where(first, pltpu.roll(s_im, 1, 1), 0.0)], axis=0)

    dcol = jnp.concatenate([d_ref[0]] * t, axis=0)
    y = _dot(mt_ref[...], ub) + _dot(q_ref[...], prev.astype(BF16)) + dcol * u
    y_ref[0] = jax.nn.gelu(y).reshape(t, c, n)


def _s5_core(u4, ldt, arc, aic, b_re, b_im, c_re, c_im, dcol, expand, tile, j_layer, *, n_chunks):
    groups, t, c, n = u4.shape
    ns = S5_STATE
    body = functools.partial(_s5_core_body, n_chunks=n_chunks)

    def per_group(shape):
        return pl.BlockSpec((None, 1) + shape, lambda g: (j_layer, g, 0, 0))

    return pl.pallas_call(
        body,
        grid=(groups,),
        in_specs=[
            pl.BlockSpec((1, t, c, n), lambda g: (g, 0, 0, 0)),
            per_group((1, 1)),
            per_group((ns, 1)), per_group((ns, 1)),
            per_group((ns, c)), per_group((ns, c)),
            per_group((c, ns)), per_group((c, ns)),
            per_group((c, 1)),
            pl.BlockSpec((t, t * c), lambda g: (0, 0)),
            pl.BlockSpec((c, t * c), lambda g: (0, 0)),
        ],
        out_specs=pl.BlockSpec((1, t, c, n), lambda g: (g, 0, 0, 0)),
        out_shape=jax.ShapeDtypeStruct((groups, t, c, n), F32),
        scratch_shapes=[pltpu.VMEM((t * c, t * c), BF16), pltpu.VMEM((t * c, 2 * ns), BF16)],
        compiler_params=_cparams("parallel"),
        name="s5_core",
    )(u4, ldt, arc, aic, b_re, b_im, c_re, c_im, dcol, expand, tile)


def _s5_outproj_body(x_ref, y_ref, qm_ref, k_ref, v_ref, qgain_ref, wglut_ref, woutt_ref, o_ref,
                     *, n_mem, nc):
    steps, n, d = x_ref.shape
    b = n // nc
    groups, _, c, _ = y_ref.shape
    for s in range(steps):
        yt = y_ref[:, s].reshape(groups * c, n)
        gate = _dot(wglut_ref[...], yt.astype(BF16))
        tok_t = (yt * jax.nn.sigmoid(gate)).astype(BF16)
        qm = qm_ref[s].astype(F32)
        cross = jnp.concatenate(
            [_cross_attention(qm[i * nc:(i + 1) * nc], k_ref[i * n_mem:(i + 1) * n_mem],
                              v_ref[i * n_mem:(i + 1) * n_mem], qgain_ref[...])
             for i in range(b)], axis=0)
        mixed_t = jnp.concatenate([tok_t, cross.T.astype(BF16)], axis=0)
        upd = _dot(woutt_ref[...], mixed_t).T
        o_ref[s] = x_ref[s] + upd


def _s5_outproj(xc, y4, qm, kmem, vmem, qgain, wglut, woutt, layer, j_layer, *, n_chunks, steps):
    t, n, d = xc.shape
    groups, _, c, _ = y4.shape
    tok_w = groups * c
    b = n // n_chunks
    n_mem = kmem.shape[0] // b
    body = functools.partial(_s5_outproj_body, n_mem=n_mem, nc=n_chunks)
    return pl.pallas_call(
        body,
        grid=(t // steps,),
        in_specs=[
            pl.BlockSpec((steps, n, d), lambda i: (i, 0, 0)),
            pl.BlockSpec((groups, steps, c, n), lambda i: (0, i, 0, 0)),
            pl.BlockSpec((steps, n, MEM_W), lambda i: (i, 0, 0)),
            pl.BlockSpec((b * n_mem, MEM_W), lambda i: (0, 0)),
            pl.BlockSpec((b * n_mem, MEM_W), lambda i: (0, 0)),
            pl.BlockSpec((None, 1, HEAD_DIM), lambda i: (layer, 0, 0)),
            pl.BlockSpec((None, tok_w, tok_w), lambda i: (j_layer, 0, 0)),
            pl.BlockSpec((None, d, d), lambda i: (layer, 0, 0)),
        ],
        out_specs=pl.BlockSpec((steps, n, d), lambda i: (i, 0, 0)),
        out_shape=jax.ShapeDtypeStruct(xc.shape, F32),
        compiler_params=_cparams("parallel"),
        name="s5_outproj",
    )(xc, y4, qm, kmem, vmem, qgain, wglut, woutt)


FFN_TM = 512
FFN_TF = 512
PROJ_TM = 1024
PROJ_TN = 512
OUT_TM = 512
S5_OUT_STEPS = 2
SB_TQ = 512
SB_TK = 256
SB_HEADS_PER_STEP = 4


def kernel(x, mem, ffn1_norm, ffn1_w_gu, ffn1_w_down, mix_norm, mem_norm, w_mem_kv, xq_norm,
           xk_norm, w_out, ffn2_norm, ffn2_w_gu, ffn2_w_down, sb_w_in, s5_w_in, s5_log_dt,
           s5_a_re, s5_a_im, s5_b_re, s5_b_im, s5_c_re, s5_c_im, s5_d, s5_w_glu):
    batch, seq, d = x.shape
    depth = ffn1_norm.shape[0]
    n_mem = mem.shape[1]
    tok_w = d - MEM_W
    heads = tok_w // HEAD_DIM
    groups = tok_w // S5_GROUP
    n_chunks = seq // S5_CHUNK
    n_b = s5_w_in.shape[0]
    ns, c, t = S5_STATE, S5_GROUP, S5_CHUNK

    def row(g):
        return g.reshape(g.shape[0], 1, g.shape[1])

    ffn1_norm, ffn2_norm, mix_norm, mem_norm = map(row, (ffn1_norm, ffn2_norm, mix_norm, mem_norm))
    xq_norm, xk_norm = row(xq_norm), row(xk_norm)
    w_kv = w_mem_kv.astype(BF16)
    w_out_b = w_out.astype(BF16)
    w_out_t = jnp.swapaxes(w_out, 1, 2).astype(BF16)
    sb_w = sb_w_in.astype(BF16)
    s5_wu_t = jnp.swapaxes(s5_w_in[:, :, :tok_w], 1, 2).astype(BF16)
    s5_wq = s5_w_in[:, :, tok_w:].astype(BF16)
    s5_wglu_t = jnp.swapaxes(s5_w_glu, 1, 2).astype(BF16)

    ldt = s5_log_dt.reshape(n_b, groups, 1, 1)
    arc, aic = s5_a_re.reshape(n_b, groups, ns, 1), s5_a_im.reshape(n_b, groups, ns, 1)
    dcol = s5_d.reshape(n_b, groups, c, 1)
    lane = jnp.arange(t * c)
    expand = (lane[None, :] // c == jnp.arange(t)[:, None]).astype(BF16)
    tile = (lane[None, :] % c == jnp.arange(c)[:, None]).astype(BF16)

    mem2 = mem.reshape(batch * n_mem, d)
    xs = x.reshape(batch * seq, d)
    ffn = functools.partial(_ffn, tm=FFN_TM, tf=FFN_TF, chunk=t)
    w_ffn = (ffn1_w_gu[0].astype(BF16), ffn1_w_down[0].astype(BF16))
    for layer in range(depth):
        kmem, vmem = _mem_kv(mem2, mem_norm, w_kv, xk_norm, layer)
        j_layer = layer // 2
        after_ffn1 = (ffn2_w_gu, ffn2_w_down, layer)
        after_ffn2 = (ffn1_w_gu, ffn1_w_down, layer + 1) if layer + 1 < depth else None
        if layer % 2 == 0:
            xs, w_ffn = ffn(xs, ffn1_norm, *w_ffn, layer, cast_next=after_ffn1)
            qkv = _sb_inproj(xs, mix_norm, sb_w, layer, j_layer, tm=PROJ_TM, tn=PROJ_TN,
                             tok_w=tok_w)
            tok = _sb_attention(qkv, batch=batch, seq=seq, heads=heads, tq=SB_TQ, tk=SB_TK,
                                hps=SB_HEADS_PER_STEP)
            xs = _sb_outproj(xs, tok, qkv, kmem, vmem, xq_norm, w_out_b, layer, tm=OUT_TM,
                             seq=seq, tok_w=tok_w)
            xs, w_ffn = ffn(xs, ffn2_norm, *w_ffn, layer, cast_next=after_ffn2)
        else:
            xc, w_ffn = ffn(xs, ffn1_norm, *w_ffn, layer, layout="to_chunked",
                            cast_next=after_ffn1)
            u4, qm = _s5_inproj(xc, mix_norm, s5_wu_t, s5_wq, layer, j_layer)
            y4 = _s5_core(u4, ldt, arc, aic, s5_b_re, s5_b_im, s5_c_re, s5_c_im, dcol,
                          expand, tile, j_layer, n_chunks=n_chunks)
            xc = _s5_outproj(xc, y4, qm, kmem, vmem, xq_norm, s5_wglu_t, w_out_t, layer, j_layer,
                             n_chunks=n_chunks, steps=S5_OUT_STEPS)
            xs, w_ffn = ffn(xc, ffn2_norm, *w_ffn, layer, layout="from_chunked",
                            cast_next=after_ffn2)
    return xs.reshape(batch, seq, d)
```

```python
import functools
import math

import jax
import jax.numpy as jnp
from jax import lax
from jax.experimental import pallas as pl
from jax.experimental.pallas import tpu as pltpu

F32 = jnp.float32
BF16 = jnp.bfloat16

EPS = 1e-6
HEAD_DIM = 128
MEM_HEADS = 4
MEM_W = MEM_HEADS * HEAD_DIM
S5_GROUP = 16
S5_STATE = 64
S5_CHUNK = 64
SWAP_SLAB = 256
F32_EXP2_UNDERFLOW = 151.0

V7X_VMEM_LIMIT_BYTES = 62 * 1024 * 1024


def _cparams(*sem):
    return pltpu.CompilerParams(dimension_semantics=sem, vmem_limit_bytes=V7X_VMEM_LIMIT_BYTES)


def _resident(block_shape, index_map):
    return pl.BlockSpec(block_shape, index_map, pipeline_mode=pl.Buffered(1))


def _rms(x, gain):
    return x * lax.rsqrt(jnp.mean(x * x, axis=-1, keepdims=True) + EPS) * gain


def _dot(a, b):
    return jnp.dot(a, b, preferred_element_type=F32)


def _dot_nt(a, b):
    return lax.dot_general(a, b, (((1,), (1,)), ((), ())), preferred_element_type=F32)


def _split3(x):
    h = x.astype(BF16)
    r = x - h.astype(F32)
    m = r.astype(BF16)
    l = (r - m.astype(F32)).astype(BF16)
    return h, m, l


def _dot_sel(x, sel):
    h, m, l = _split3(x)
    return _dot(h, sel) + _dot(m, sel) + _dot(l, sel)


def _dot3(a, b):
    ah = a.astype(BF16)
    al = (a - ah.astype(F32)).astype(BF16)
    bh = b.astype(BF16)
    bl = (b - bh.astype(F32)).astype(BF16)
    return _dot(ah, bh) + _dot(ah, bl) + _dot(al, bh)


def _ffn_body(*refs, layout, chunk, cast_next):
    n_extra = 2 if cast_next else 0
    x_ref, gain_ref, wg_ref, wu_ref, wd_ref = refs[:5]
    next_in = refs[5:5 + n_extra]
    o_ref = refs[5 + n_extra]
    next_out = refs[6 + n_extra:6 + 2 * n_extra]
    h_ref = refs[6 + 2 * n_extra]

    j = pl.program_id(1)
    tm, d = h_ref.shape
    t = chunk
    cpb = tm // chunk
    slabs = [slice(s, s + SWAP_SLAB) for s in range(0, d, SWAP_SLAB)]

    @pl.when(j == 0)
    def _():
        if layout == "to_chunked":
            for sl in slabs:
                o_ref[:, :, sl] = jnp.swapaxes(x_ref[:, sl].reshape(cpb, t, SWAP_SLAB), 0, 1)
            x = o_ref[...].reshape(tm, d)
        else:
            x = x_ref[...].reshape(tm, d)
            o_ref[...] = x
        h_ref[...] = _rms(x, gain_ref[...]).astype(BF16)

    h = h_ref[...]
    g = _dot(h, wg_ref[...])
    u = _dot(h, wu_ref[...])
    a = (g * jax.nn.sigmoid(g) * (u * 0.5)).astype(BF16)
    o_ref[...] += _dot(a, wd_ref[...]).reshape(o_ref.shape)

    for src, dst in zip(next_in, next_out):
        dst[...] = src[...].astype(BF16)

    if layout == "from_chunked":
        @pl.when(j == pl.num_programs(1) - 1)
        def _():
            for sl in slabs:
                rows = o_ref[:, sl].reshape(t, cpb, SWAP_SLAB)
                o_ref[:, sl] = jnp.swapaxes(rows, 0, 1).reshape(tm, SWAP_SLAB)


def _ffn(x, gain, w_gu, w_down, layer, *, tm, tf, layout="rows", chunk=None, cast_next=None):
    d = x.shape[-1]
    m = x.size // d
    f = w_down.shape[0]
    nf = f // tf
    n_i = m // tm
    rows_spec = pl.BlockSpec((tm, d), lambda i, j: (i, 0))
    rows_shape = jax.ShapeDtypeStruct((m, d), F32)
    scratch = [pltpu.VMEM((tm, d), BF16)]
    if layout == "rows":
        x_spec, o_spec, o_shape = rows_spec, rows_spec, rows_shape
    else:
        cpb = tm // chunk
        chunk_spec = pl.BlockSpec((chunk, cpb, d), lambda i, j: (0, i, 0))
        chunk_shape = jax.ShapeDtypeStruct((chunk, m // chunk, d), F32)
        if layout == "to_chunked":
            x_spec, o_spec, o_shape = rows_spec, chunk_spec, chunk_shape
        else:
            x_spec, o_spec, o_shape = chunk_spec, rows_spec, rows_shape
    in_specs = [
        x_spec,
        pl.BlockSpec((None, 1, d), lambda i, j: (layer, 0, 0)),
        pl.BlockSpec((d, tf), lambda i, j: (0, j)),
        pl.BlockSpec((d, tf), lambda i, j: (0, nf + j)),
        pl.BlockSpec((tf, d), lambda i, j: (j, 0)),
    ]
    args = [x, gain, w_gu, w_gu, w_down]
    out_specs, out_shapes = [o_spec], [o_shape]
    if cast_next is not None:
        gu_f32, down_f32, nxt = cast_next
        gu_rows, gu_cols, down_rows = d // n_i, 2 * f // nf, f // (n_i * nf)
        assert gu_rows * n_i == d and gu_cols * nf == 2 * f and down_rows * n_i * nf == f
        in_specs += [pl.BlockSpec((None, gu_rows, gu_cols), lambda i, j: (nxt, i, j)),
                     pl.BlockSpec((None, down_rows, d), lambda i, j: (nxt, i * nf + j, 0))]
        args += [gu_f32, down_f32]
        out_specs += [pl.BlockSpec((gu_rows, gu_cols), lambda i, j: (i, j)),
                      pl.BlockSpec((down_rows, d), lambda i, j: (i * nf + j, 0))]
        out_shapes += [jax.ShapeDtypeStruct((d, 2 * f), BF16), jax.ShapeDtypeStruct((f, d), BF16)]
    outs = pl.pallas_call(
        functools.partial(_ffn_body, layout=layout, chunk=chunk, cast_next=cast_next is not None),
        grid=(n_i, nf),
        in_specs=in_specs,
        out_specs=out_specs,
        out_shape=out_shapes,
        scratch_shapes=scratch,
        compiler_params=_cparams("parallel", "arbitrary"),
        name="ffn_" + layout,
    )(*args)
    return outs[0], tuple(outs[1:])


def _sb_inproj_body(x_ref, gain_ref, w_ref, o_ref, h_ref, *, n_q_blocks, q_scale):
    j = pl.program_id(1)

    @pl.when(j == 0)
    def _():
        h_ref[...] = _rms(x_ref[...], gain_ref[...]).astype(BF16)

    acc = _dot(h_ref[...], w_ref[...])
    scale = jnp.where(j < n_q_blocks, q_scale, 1.0).astype(F32)
    o_ref[...] = (acc * scale).astype(o_ref.dtype)


def _sb_inproj(x, gain, w_in, layer, j_layer, *, tm, tn, tok_w):
    m, d = x.shape
    n = w_in.shape[2]
    body = functools.partial(_sb_inproj_body, n_q_blocks=tok_w // tn,
                             q_scale=math.log2(math.e) / math.sqrt(HEAD_DIM))
    return pl.pallas_call(
        body,
        grid=(m // tm, n // tn),
        in_specs=[
            pl.BlockSpec((tm, d), lambda i, j: (i, 0)),
            pl.BlockSpec((None, 1, d), lambda i, j: (layer, 0, 0)),
            pl.BlockSpec((None, d, tn), lambda i, j: (j_layer, 0, j)),
        ],
        out_specs=pl.BlockSpec((tm, tn), lambda i, j: (i, j)),
        out_shape=jax.ShapeDtypeStruct((m, n), BF16),
        scratch_shapes=[pltpu.VMEM((tm, d), BF16)],
        compiler_params=_cparams("parallel", "arbitrary"),
        name="sb_inproj",
    )(x, gain, w_in)


def _sb_attn_body(q_ref, k_ref, v_ref, o_ref, acc_ref, carry_ref, *, tq, tk, hps):
    qi = pl.program_id(2)
    diag_blocks = tq // tk
    row = lax.broadcasted_iota(jnp.int32, (tk, tk), 0)
    col = lax.broadcasted_iota(jnp.int32, (tk, tk), 1)
    suffix = (row > col).astype(BF16)
    q_pos = lax.broadcasted_iota(jnp.int32, (tq, tk), 0)
    k_pos = lax.broadcasted_iota(jnp.int32, (tq, tk), 1)

    def visit(kb, diag_index):
        start = pl.multiple_of(kb * tk, tk)
        heads = [slice(hd * HEAD_DIM, (hd + 1) * HEAD_DIM) for hd in range(hps)]
        if diag_index is not None:
            before = (k_pos + diag_index * tk) < q_pos
        zs, sps = [], []
        for sl in heads:
            z = _dot_nt(q_ref[:, sl], k_ref[pl.ds(start, tk), sl])
            neg_abs = lax.bitcast_convert_type(
                lax.bitcast_convert_type(z, jnp.uint32) | jnp.uint32(0x80000000), F32)
            sp = jnp.maximum(z, 0.0) + jnp.log2(1.0 + jnp.exp2(neg_abs))
            if diag_index is not None:
                sp = jnp.where(before, sp, 0.0)
            zs.append(z)
            sps.append(sp)
        spbs = [sp.astype(BF16) for sp in sps]
        inner_all = _dot(jnp.concatenate(spbs, axis=0), suffix)
        for hd, sl in enumerate(heads):
            inner = inner_all[hd * tq:(hd + 1) * tq]
            total = inner[:, 0:1] + spbs[hd][:, 0:1].astype(F32)
            carry = carry_ref[:, sl]
            carry_k = jnp.concatenate([carry] * (tk // HEAD_DIM), axis=1)
            w = jnp.exp2((zs[hd] - sps[hd]) - inner - carry_k)
            if diag_index is not None:
                w = jnp.where(before, w, 0.0)
            acc_ref[:, sl] += _dot(w.astype(BF16), v_ref[pl.ds(start, tk), sl])
            carry_ref[:, sl] = carry + jnp.broadcast_to(total, carry.shape)

    acc_ref[...] = jnp.zeros_like(acc_ref)
    carry_ref[...] = jnp.zeros_like(carry_ref)
    for j in reversed(range(diag_blocks)):
        visit(qi * diag_blocks + j, j)

    n_far = qi * diag_blocks

    def more(state):
        i, least_carry = state
        return jnp.logical_and(i < n_far, least_carry < F32_EXP2_UNDERFLOW)

    def step(state):
        i, _ = state
        visit(n_far - 1 - i, None)
        return i + 1, jnp.min(carry_ref[...])

    lax.while_loop(more, step, (jnp.int32(0), jnp.min(carry_ref[...])))
    o_ref[...] = acc_ref[...].astype(o_ref.dtype)


def _sb_attention(qkv, *, batch, seq, heads, tq, tk, hps):
    nq = seq // tq
    hg = heads // hps
    w = hps * HEAD_DIM
    body = functools.partial(_sb_attn_body, tq=tq, tk=tk, hps=hps)
    return pl.pallas_call(
        body,
        grid=(batch, hg, nq),
        in_specs=[
            pl.BlockSpec((tq, w), lambda b, h, i: (b * nq + i, h)),
            pl.BlockSpec((seq, w), lambda b, h, i: (b, hg + h)),
            pl.BlockSpec((seq, w), lambda b, h, i: (b, 2 * hg + h)),
        ],
        out_specs=pl.BlockSpec((tq, w), lambda b, h, i: (b * nq + i, h)),
        out_shape=jax.ShapeDtypeStruct((batch * seq, heads * HEAD_DIM), BF16),
        scratch_shapes=[pltpu.VMEM((tq, w), F32), pltpu.VMEM((tq, w), F32)],
        compiler_params=_cparams("parallel", "parallel", "arbitrary"),
        name="sb_attention",
    )(qkv, qkv, qkv)


def _mem_kv_body(mem_ref, gain_ref, w_ref, kgain_ref, k_ref, v_ref):
    h = _rms(mem_ref[...], gain_ref[...]).astype(BF16)
    kv = _dot(h, w_ref[...])
    for hd in range(MEM_HEADS):
        sl = slice(hd * HEAD_DIM, (hd + 1) * HEAD_DIM)
        k_ref[:, sl] = _rms(kv[:, sl], kgain_ref[...]).astype(BF16)
    v_ref[...] = kv[:, MEM_W:].astype(BF16)


def _mem_kv(mem, gain, w_kv, kgain, layer):
    rows, d = mem.shape
    return pl.pallas_call(
        _mem_kv_body,
        grid=(1,),
        in_specs=[
            pl.BlockSpec((rows, d), lambda i: (0, 0)),
            pl.BlockSpec((None, 1, d), lambda i: (layer, 0, 0)),
            pl.BlockSpec((None, d, 2 * MEM_W), lambda i: (layer, 0, 0)),
            pl.BlockSpec((None, 1, HEAD_DIM), lambda i: (layer, 0, 0)),
        ],
        out_specs=[pl.BlockSpec((rows, MEM_W), lambda i: (0, 0))] * 2,
        out_shape=[jax.ShapeDtypeStruct((rows, MEM_W), BF16)] * 2,
        compiler_params=_cparams("arbitrary"),
        name="mem_kv",
    )(mem, gain, w_kv, kgain)


def _cross_attention(qm, k, v, qgain):
    outs = []
    for hd in range(MEM_HEADS):
        sl = slice(hd * HEAD_DIM, (hd + 1) * HEAD_DIM)
        qn = _rms(qm[:, sl], qgain).astype(BF16)
        s = _dot_nt(qn, k[:, sl]) * (1.0 / math.sqrt(HEAD_DIM))
        p = jnp.exp(s - jnp.max(s, axis=-1, keepdims=True))
        p = p / jnp.sum(p, axis=-1, keepdims=True)
        outs.append(_dot(p.astype(BF16), v[:, sl]))
    return jnp.concatenate(outs, axis=1)


def _sb_outproj_body(x_ref, tok_ref, qm_ref, k_ref, v_ref, qgain_ref, w1_ref, w2_ref, o_ref):
    cross = _cross_attention(qm_ref[...].astype(F32), k_ref[...], v_ref[...], qgain_ref[...])
    o_ref[...] = (x_ref[...] + _dot(tok_ref[...], w1_ref[...])
                  + _dot(cross.astype(BF16), w2_ref[...]))


def _sb_outproj(x, tok, qkv, kmem, vmem, qgain, w_out, layer, j_layer, *, tm, seq, tok_w):
    m, d = x.shape
    n_mem = kmem.shape[0] // (m // seq)
    per_batch = seq // tm
    qm_block = tok_w * 3 // MEM_W
    return pl.pallas_call(
        _sb_outproj_body,
        grid=(m // tm,),
        in_specs=[
            pl.BlockSpec((tm, d), lambda i: (i, 0)),
            pl.BlockSpec((tm, tok_w), lambda i: (i, 0)),
            pl.BlockSpec((tm, MEM_W), lambda i: (i, qm_block)),
            pl.BlockSpec((n_mem, MEM_W), lambda i: (i // per_batch, 0)),
            pl.BlockSpec((n_mem, MEM_W), lambda i: (i // per_batch, 0)),
            pl.BlockSpec((None, 1, HEAD_DIM), lambda i: (layer, 0, 0)),
            _resident((None, tok_w, d), lambda i: (j_layer, 0, 0)),
            _resident((None, MEM_W, d), lambda i: (j_layer, tok_w // MEM_W, 0)),
        ],
        out_specs=pl.BlockSpec((tm, d), lambda i: (i, 0)),
        out_shape=jax.ShapeDtypeStruct((m, d), F32),
        compiler_params=_cparams("parallel"),
        name="sb_outproj",
    )(x, tok, qkv, kmem, vmem, qgain, w_out, w_out)


def _s5_inproj_body(x_ref, gain_ref, wut_ref, wq_ref, u_ref, qm_ref):
    h = _rms(x_ref[...], gain_ref[...]).astype(BF16)
    ut = _dot_nt(wut_ref[...], h)
    u_ref[...] = ut.reshape(u_ref.shape)
    qm_ref[0] = _dot(h, wq_ref[...]).astype(qm_ref.dtype)


def _s5_inproj(xc, gain, wut, wq, layer, j_layer):
    t, n, d = xc.shape
    tok_w = wut.shape[1]
    groups = tok_w // S5_GROUP
    return pl.pallas_call(
        _s5_inproj_body,
        grid=(t,),
        in_specs=[
            pl.BlockSpec((None, n, d), lambda i: (i, 0, 0)),
            pl.BlockSpec((None, 1, d), lambda i: (layer, 0, 0)),
            _resident((None, tok_w, d), lambda i: (j_layer, 0, 0)),
            _resident((None, d, MEM_W), lambda i: (j_layer, 0, 0)),
        ],
        out_specs=[
            pl.BlockSpec((groups, 1, S5_GROUP, n), lambda i: (0, i, 0, 0)),
            pl.BlockSpec((1, n, MEM_W), lambda i: (i, 0, 0)),
        ],
        out_shape=[
            jax.ShapeDtypeStruct((groups, t, S5_GROUP, n), F32),
            jax.ShapeDtypeStruct((t, n, MEM_W), BF16),
        ],
        compiler_params=_cparams("parallel"),
        name="s5_inproj",
    )(xc, gain, wut, wq)


def _cexp(zr, zi, k):
    mag = jnp.exp(zr * k)
    return mag * jnp.cos(zi * k), mag * jnp.sin(zi * k)


def _s5_core_body(u_ref, ldt_ref, arc_ref, aic_ref, br_ref, bi_ref,
                  cr_ref, ci_ref, d_ref, exp_ref, tile_ref, y_ref, mt_ref, q_ref, *, n_chunks):
    for g in range(u_ref.shape[0]):
        _s5_core_group(g, u_ref, ldt_ref, arc_ref, aic_ref, br_ref, bi_ref, cr_ref, ci_ref,
                       d_ref, exp_ref, tile_ref, y_ref, mt_ref, q_ref, n_chunks=n_chunks)


def _s5_core_group(g, u_ref, ldt_ref, arc_ref, aic_ref, br_ref, bi_ref,
                   cr_ref, ci_ref, d_ref, exp_ref, tile_ref, y_ref, mt_ref, q_ref, *, n_chunks):
    t = S5_CHUNK
    c = S5_GROUP
    ns = S5_STATE
    n = u_ref.shape[-1]
    dt = jnp.exp(ldt_ref[g])

    lr, li = arc_ref[g], aic_ref[g]
    zr, zi = lr * dt, li * dt
    back = (t - 1 - lax.broadcasted_iota(jnp.int32, (ns, t), 1)).astype(F32)
    pwr, pwi = _cexp(zr, zi, back)
    abr, abi = pwr[:, t - 2:t - 1], pwi[:, t - 2:t - 1]
    top_r, top_i = pwr[:, 0:1], pwi[:, 0:1]
    mr, mi = top_r * abr - top_i * abi, top_r * abi + top_i * abr
    pwr_t, pwi_t = pwr.T, pwi.T

    nr, ni = abr - 1.0, abi
    den = lr * lr + li * li
    fr = (nr * lr + ni * li) / den
    fi = (ni * lr - nr * li) / den
    b_re, b_im = br_ref[g], bi_ref[g]
    bbr = fr * b_re - fi * b_im
    bbi = fr * b_im + fi * b_re

    pwr_x, pwi_x = _dot_sel(pwr, exp_ref[...]), _dot_sel(pwi, exp_ref[...])
    bbr_x, bbi_x = _dot_sel(bbr, tile_ref[...]), _dot_sel(bbi, tile_ref[...])
    p_re = pwr_x * bbr_x - pwi_x * bbi_x
    p_im = pwr_x * bbi_x + pwi_x * bbr_x
    p = jnp.concatenate([p_re, p_im], axis=0)

    c_re, c_im = cr_ref[g], ci_ref[g]
    c_ext = jnp.concatenate([c_re, -c_im], axis=1)
    k_rev = jnp.concatenate([_dot3(c_ext, p), jnp.zeros((c, t * c), F32)], axis=1)
    width = 2 * t * c
    for step in range(t):
        off = (t - 1 - step) * c
        win = k_rev if off == 0 else pltpu.roll(k_rev, width - off, 1)
        mt_ref[g, step * c:(step + 1) * c, :] = win[:, :t * c].astype(BF16)

    for step in range(t):
        if step < t - 1:
            src = t - 2 - step
            ar, ai = pwr_t[src:src + 1, :], pwi_t[src:src + 1, :]
        else:
            hr, hi, lr_, li_ = pwr_t[0:1], pwi_t[0:1], pwr_t[t - 2:t - 1], pwi_t[t - 2:t - 1]
            ar, ai = hr * lr_ - hi * li_, hr * li_ + hi * lr_
        q_ref[g, step * c:(step + 1) * c, 0:ns] = (c_re * ar - c_im * ai).astype(BF16)
        q_ref[g, step * c:(step + 1) * c, ns:2 * ns] = (-(c_re * ai + c_im * ar)).astype(BF16)

    u = u_ref[g].reshape(t * c, n)
    ub = u.astype(BF16)
    inj = _dot(p.astype(BF16), ub)
    s_re, s_im = inj[:ns], inj[ns:]

    pos = lax.broadcasted_iota(jnp.int32, (ns, n), 1) % n_chunks
    shift = 1
    while shift < n_chunks:
        keep = pos >= shift
        sh_re = jnp.where(keep, pltpu.roll(s_re, shift, 1), 0.0)
        sh_im = jnp.where(keep, pltpu.roll(s_im, shift, 1), 0.0)
        s_re, s_im = s_re + mr * sh_re - mi * sh_im, s_im + mr * sh_im + mi * sh_re
        mr, mi = mr * mr - mi * mi, 2.0 * mr * mi
        shift *= 2
    first = pos >= 1
    prev = jnp.concatenate([jnp.where(first, pltpu.roll(s_re, 1, 1), 0.0),
                            jnp.where(first, pltpu.roll(s_im, 1, 1), 0.0)], axis=0)

    dcol = jnp.concatenate([d_ref[g]] * t, axis=0)
    y = _dot(mt_ref[g], ub) + _dot(q_ref[g], prev.astype(BF16)) + dcol * u
    y_ref[g] = jax.nn.gelu(y).reshape(t, c, n)


def _s5_core(u4, ldt, arc, aic, b_re, b_im, c_re, c_im, dcol, expand, tile, j_layer, *, n_chunks,
             gps):
    groups, t, c, n = u4.shape
    ns = S5_STATE
    body = functools.partial(_s5_core_body, n_chunks=n_chunks)

    def per_group(shape):
        return pl.BlockSpec((None, gps) + shape, lambda g: (j_layer, g, 0, 0))

    return pl.pallas_call(
        body,
        grid=(groups // gps,),
        in_specs=[
            pl.BlockSpec((gps, t, c, n), lambda g: (g, 0, 0, 0)),
            per_group((1, 1)),
            per_group((ns, 1)), per_group((ns, 1)),
            per_group((ns, c)), per_group((ns, c)),
            per_group((c, ns)), per_group((c, ns)),
            per_group((c, 1)),
            pl.BlockSpec((t, t * c), lambda g: (0, 0)),
            pl.BlockSpec((c, t * c), lambda g: (0, 0)),
        ],
        out_specs=pl.BlockSpec((gps, t, c, n), lambda g: (g, 0, 0, 0)),
        out_shape=jax.ShapeDtypeStruct((groups, t, c, n), F32),
        scratch_shapes=[pltpu.VMEM((gps, t * c, t * c), BF16),
                        pltpu.VMEM((gps, t * c, 2 * ns), BF16)],
        compiler_params=_cparams("parallel"),
        name="s5_core",
    )(u4, ldt, arc, aic, b_re, b_im, c_re, c_im, dcol, expand, tile)


def _s5_outproj_body(x_ref, y_ref, qm_ref, k_ref, v_ref, qgain_ref, wglut_ref, woutt_ref, o_ref,
                     *, n_mem, nc):
    steps, n, d = x_ref.shape
    b = n // nc
    groups, _, c, _ = y_ref.shape
    for s in range(steps):
        yt = y_ref[:, s].reshape(groups * c, n)
        gate = _dot(wglut_ref[...], yt.astype(BF16))
        tok_t = (yt * jax.nn.sigmoid(gate)).astype(BF16)
        qm = qm_ref[s].astype(F32)
        cross = jnp.concatenate(
            [_cross_attention(qm[i * nc:(i + 1) * nc], k_ref[i * n_mem:(i + 1) * n_mem],
                              v_ref[i * n_mem:(i + 1) * n_mem], qgain_ref[...])
             for i in range(b)], axis=0)
        mixed_t = jnp.concatenate([tok_t, cross.T.astype(BF16)], axis=0)
        upd = _dot(woutt_ref[...], mixed_t).T
        o_ref[s] = x_ref[s] + upd


def _s5_outproj(xc, y4, qm, kmem, vmem, qgain, wglut, woutt, layer, j_layer, *, n_chunks, steps):
    t, n, d = xc.shape
    groups, _, c, _ = y4.shape
    tok_w = groups * c
    b = n // n_chunks
    n_mem = kmem.shape[0] // b
    body = functools.partial(_s5_outproj_body, n_mem=n_mem, nc=n_chunks)
    return pl.pallas_call(
        body,
        grid=(t // steps,),
        in_specs=[
            pl.BlockSpec((steps, n, d), lambda i: (i, 0, 0)),
            pl.BlockSpec((groups, steps, c, n), lambda i: (0, i, 0, 0)),
            pl.BlockSpec((steps, n, MEM_W), lambda i: (i, 0, 0)),
            pl.BlockSpec((b * n_mem, MEM_W), lambda i: (0, 0)),
            pl.BlockSpec((b * n_mem, MEM_W), lambda i: (0, 0)),
            pl.BlockSpec((None, 1, HEAD_DIM), lambda i: (layer, 0, 0)),
            _resident((None, tok_w, tok_w), lambda i: (j_layer, 0, 0)),
            _resident((None, d, d), lambda i: (j_layer, 0, 0)),
        ],
        out_specs=pl.BlockSpec((steps, n, d), lambda i: (i, 0, 0)),
        out_shape=jax.ShapeDtypeStruct(xc.shape, F32),
        compiler_params=_cparams("parallel"),
        name="s5_outproj",
    )(xc, y4, qm, kmem, vmem, qgain, wglut, woutt)


FFN_TM = {"rows": 1024, "from_chunked": 1024, "to_chunked": 512}
FFN_TF = 512
PROJ_TM = 1024
PROJ_TN = 512
OUT_TM = 512
S5_OUT_STEPS = 2
S5_GROUPS_PER_STEP = 2
SB_TQ = 512
SB_TK = 256
SB_HEADS_PER_STEP = 4


def kernel(x, mem, ffn1_norm, ffn1_w_gu, ffn1_w_down, mix_norm, mem_norm, w_mem_kv, xq_norm,
           xk_norm, w_out, ffn2_norm, ffn2_w_gu, ffn2_w_down, sb_w_in, s5_w_in, s5_log_dt,
           s5_a_re, s5_a_im, s5_b_re, s5_b_im, s5_c_re, s5_c_im, s5_d, s5_w_glu):
    batch, seq, d = x.shape
    depth = ffn1_norm.shape[0]
    n_mem = mem.shape[1]
    tok_w = d - MEM_W
    heads = tok_w // HEAD_DIM
    groups = tok_w // S5_GROUP
    n_chunks = seq // S5_CHUNK
    n_b = s5_w_in.shape[0]
    ns, c, t = S5_STATE, S5_GROUP, S5_CHUNK

    def row(g):
        return g.reshape(g.shape[0], 1, g.shape[1])

    ffn1_norm, ffn2_norm, mix_norm, mem_norm = map(row, (ffn1_norm, ffn2_norm, mix_norm, mem_norm))
    xq_norm, xk_norm = row(xq_norm), row(xk_norm)
    w_kv = w_mem_kv.astype(BF16)
    w_out_b = w_out[0::2].astype(BF16)
    w_out_t = jnp.swapaxes(w_out[1::2], 1, 2).astype(BF16)
    sb_w = sb_w_in.astype(BF16)
    s5_wu_t = jnp.swapaxes(s5_w_in[:, :, :tok_w], 1, 2).astype(BF16)
    s5_wq = s5_w_in[:, :, tok_w:].astype(BF16)
    s5_wglu_t = jnp.swapaxes(s5_w_glu, 1, 2).astype(BF16)

    ldt = s5_log_dt.reshape(n_b, groups, 1, 1)
    arc, aic = s5_a_re.reshape(n_b, groups, ns, 1), s5_a_im.reshape(n_b, groups, ns, 1)
    dcol = s5_d.reshape(n_b, groups, c, 1)
    lane = jnp.arange(t * c)
    expand = (lane[None, :] // c == jnp.arange(t)[:, None]).astype(BF16)
    tile = (lane[None, :] % c == jnp.arange(c)[:, None]).astype(BF16)

    mem2 = mem.reshape(batch * n_mem, d)
    xs = x.reshape(batch * seq, d)
    def ffn(xin, gain, w_gu, w_down, layer, layout="rows", cast_next=None):
        tm = min(FFN_TM[layout], batch * seq)
        return _ffn(xin, gain, w_gu, w_down, layer, tm=tm, tf=FFN_TF, layout=layout, chunk=t,
                    cast_next=cast_next)

    w_ffn = (ffn1_w_gu[0].astype(BF16), ffn1_w_down[0].astype(BF16))
    for layer in range(depth):
        kmem, vmem = _mem_kv(mem2, mem_norm, w_kv, xk_norm, layer)
        j_layer = layer // 2
        after_ffn1 = (ffn2_w_gu, ffn2_w_down, layer)
        after_ffn2 = (ffn1_w_gu, ffn1_w_down, layer + 1) if layer + 1 < depth else None
        if layer % 2 == 0:
            xs, w_ffn = ffn(xs, ffn1_norm, *w_ffn, layer, cast_next=after_ffn1)
            qkv = _sb_inproj(xs, mix_norm, sb_w, layer, j_layer, tm=PROJ_TM, tn=PROJ_TN,
                             tok_w=tok_w)
            tok = _sb_attention(qkv, batch=batch, seq=seq, heads=heads, tq=SB_TQ, tk=SB_TK,
                                hps=SB_HEADS_PER_STEP)
            xs = _sb_outproj(xs, tok, qkv, kmem, vmem, xq_norm, w_out_b, layer, j_layer,
                             tm=OUT_TM, seq=seq, tok_w=tok_w)
            xs, w_ffn = ffn(xs, ffn2_norm, *w_ffn, layer, cast_next=after_ffn2)
        else:
            xc, w_ffn = ffn(xs, ffn1_norm, *w_ffn, layer, layout="to_chunked",
                            cast_next=after_ffn1)
            u4, qm = _s5_inproj(xc, mix_norm, s5_wu_t, s5_wq, layer, j_layer)
            y4 = _s5_core(u4, ldt, arc, aic, s5_b_re, s5_b_im, s5_c_re, s5_c_im, dcol,
                          expand, tile, j_layer, n_chunks=n_chunks, gps=S5_GROUPS_PER_STEP)
            xc = _s5_outproj(xc, y4, qm, kmem, vmem, xq_norm, s5_wglu_t, w_out_t, layer, j_layer,
                             n_chunks=n_chunks, steps=S5_OUT_STEPS)
            xs, w_ffn = ffn(xc, ffn2_norm, *w_ffn, layer, layout="from_chunked",
                            cast_next=after_ffn2)
    return xs.reshape(batch, seq, d)
```

```python
import functools
import math

import jax
import jax.numpy as jnp
from jax import lax
from jax.experimental import pallas as pl
from jax.experimental.pallas import tpu as pltpu

F32 = jnp.float32
BF16 = jnp.bfloat16

EPS = 1e-6
HEAD_DIM = 128
MEM_HEADS = 4
MEM_W = MEM_HEADS * HEAD_DIM
S5_GROUP = 16
S5_STATE = 64
S5_CHUNK = 64
SWAP_SLAB = 256
F32_EXP2_UNDERFLOW = 151.0

V7X_VMEM_LIMIT_BYTES = 62 * 1024 * 1024


def _cparams(*sem):
    return pltpu.CompilerParams(dimension_semantics=sem, vmem_limit_bytes=V7X_VMEM_LIMIT_BYTES)


def _resident(block_shape, index_map):
    return pl.BlockSpec(block_shape, index_map, pipeline_mode=pl.Buffered(1))


def _rms(x, gain):
    return x * lax.rsqrt(jnp.mean(x * x, axis=-1, keepdims=True) + EPS) * gain


def _dot(a, b):
    return jnp.dot(a, b, preferred_element_type=F32)


def _dot_nt(a, b):
    return lax.dot_general(a, b, (((1,), (1,)), ((), ())), preferred_element_type=F32)


def _split3(x):
    h = x.astype(BF16)
    r = x - h.astype(F32)
    m = r.astype(BF16)
    l = (r - m.astype(F32)).astype(BF16)
    return h, m, l


def _dot_sel(x, sel):
    h, m, l = _split3(x)
    return _dot(h, sel) + _dot(m, sel) + _dot(l, sel)


def _dot3(a, b):
    ah = a.astype(BF16)
    al = (a - ah.astype(F32)).astype(BF16)
    bh = b.astype(BF16)
    bl = (b - bh.astype(F32)).astype(BF16)
    return _dot(ah, bh) + _dot(ah, bl) + _dot(al, bh)


def _ffn_body(*refs, layout, chunk, cast_next):
    n_extra = 2 if cast_next else 0
    x_ref, gain_ref, wg_ref, wu_ref, wd_ref = refs[:5]
    next_in = refs[5:5 + n_extra]
    o_ref = refs[5 + n_extra]
    next_out = refs[6 + n_extra:6 + 2 * n_extra]
    h_ref = refs[6 + 2 * n_extra]

    j = pl.program_id(1)
    tm, d = h_ref.shape
    t = chunk
    cpb = tm // chunk
    slabs = [slice(s, s + SWAP_SLAB) for s in range(0, d, SWAP_SLAB)]

    @pl.when(j == 0)
    def _():
        x = x_ref[...].reshape(tm, d)
        o_ref[...] = x.reshape(o_ref.shape)
        h_ref[...] = _rms(x, gain_ref[...]).astype(BF16)

    h = h_ref[...]
    g = _dot(h, wg_ref[...])
    u = _dot(h, wu_ref[...])
    a = (g * jax.nn.sigmoid(g) * (u * 0.5)).astype(BF16)
    o_ref[...] += _dot(a, wd_ref[...]).reshape(o_ref.shape)

    for src, dst in zip(next_in, next_out):
        dst[...] = src[...].astype(BF16)

    if layout != "rows":
        @pl.when(j == pl.num_programs(1) - 1)
        def _():
            for sl in slabs:
                if layout == "to_chunked":
                    rows = o_ref[:, :, sl].reshape(cpb, t, SWAP_SLAB)
                    o_ref[:, :, sl] = jnp.swapaxes(rows, 0, 1)
                else:
                    rows = o_ref[:, sl].reshape(t, cpb, SWAP_SLAB)
                    o_ref[:, sl] = jnp.swapaxes(rows, 0, 1).reshape(tm, SWAP_SLAB)


def _ffn(x, gain, w_gu, w_down, layer, *, tm, tf, layout="rows", chunk=None, cast_next=None):
    d = x.shape[-1]
    m = x.size // d
    f = w_down.shape[0]
    nf = f // tf
    n_i = m // tm
    rows_spec = pl.BlockSpec((tm, d), lambda i, j: (i, 0))
    rows_shape = jax.ShapeDtypeStruct((m, d), F32)
    scratch = [pltpu.VMEM((tm, d), BF16)]
    if layout == "rows":
        x_spec, o_spec, o_shape = rows_spec, rows_spec, rows_shape
    else:
        cpb = tm // chunk
        chunk_spec = pl.BlockSpec((chunk, cpb, d), lambda i, j: (0, i, 0))
        chunk_shape = jax.ShapeDtypeStruct((chunk, m // chunk, d), F32)
        if layout == "to_chunked":
            x_spec, o_spec, o_shape = rows_spec, chunk_spec, chunk_shape
        else:
            x_spec, o_spec, o_shape = chunk_spec, rows_spec, rows_shape
    in_specs = [
        x_spec,
        pl.BlockSpec((None, 1, d), lambda i, j: (layer, 0, 0)),
        pl.BlockSpec((d, tf), lambda i, j: (0, j)),
        pl.BlockSpec((d, tf), lambda i, j: (0, nf + j)),
        pl.BlockSpec((tf, d), lambda i, j: (j, 0)),
    ]
    args = [x, gain, w_gu, w_gu, w_down]
    out_specs, out_shapes = [o_spec], [o_shape]
    if cast_next is not None:
        gu_f32, down_f32, nxt = cast_next
        gu_rows, gu_cols, down_rows = d // n_i, 2 * f // nf, f // (n_i * nf)
        assert gu_rows * n_i == d and gu_cols * nf == 2 * f and down_rows * n_i * nf == f
        in_specs += [pl.BlockSpec((None, gu_rows, gu_cols), lambda i, j: (nxt, i, j)),
                     pl.BlockSpec((None, down_rows, d), lambda i, j: (nxt, i * nf + j, 0))]
        args += [gu_f32, down_f32]
        out_specs += [pl.BlockSpec((gu_rows, gu_cols), lambda i, j: (i, j)),
                      pl.BlockSpec((down_rows, d), lambda i, j: (i * nf + j, 0))]
        out_shapes += [jax.ShapeDtypeStruct((d, 2 * f), BF16), jax.ShapeDtypeStruct((f, d), BF16)]
    outs = pl.pallas_call(
        functools.partial(_ffn_body, layout=layout, chunk=chunk, cast_next=cast_next is not None),
        grid=(n_i, nf),
        in_specs=in_specs,
        out_specs=out_specs,
        out_shape=out_shapes,
        scratch_shapes=scratch,
        compiler_params=_cparams("parallel", "arbitrary"),
        name="ffn_" + layout,
    )(*args)
    return outs[0], tuple(outs[1:])


def _sb_inproj_body(x_ref, gain_ref, w_ref, o_ref, h_ref, *, n_q_blocks, q_scale):
    j = pl.program_id(1)

    @pl.when(j == 0)
    def _():
        h_ref[...] = _rms(x_ref[...], gain_ref[...]).astype(BF16)

    acc = _dot(h_ref[...], w_ref[...])
    scale = jnp.where(j < n_q_blocks, q_scale, 1.0).astype(F32)
    o_ref[...] = (acc * scale).astype(o_ref.dtype)


def _sb_inproj(x, gain, w_in, layer, j_layer, *, tm, tn, tok_w):
    m, d = x.shape
    n = w_in.shape[2]
    body = functools.partial(_sb_inproj_body, n_q_blocks=tok_w // tn,
                             q_scale=math.log2(math.e) / math.sqrt(HEAD_DIM))
    return pl.pallas_call(
        body,
        grid=(m // tm, n // tn),
        in_specs=[
            pl.BlockSpec((tm, d), lambda i, j: (i, 0)),
            pl.BlockSpec((None, 1, d), lambda i, j: (layer, 0, 0)),
            pl.BlockSpec((None, d, tn), lambda i, j: (j_layer, 0, j)),
        ],
        out_specs=pl.BlockSpec((tm, tn), lambda i, j: (i, j)),
        out_shape=jax.ShapeDtypeStruct((m, n), BF16),
        scratch_shapes=[pltpu.VMEM((tm, d), BF16)],
        compiler_params=_cparams("parallel", "arbitrary"),
        name="sb_inproj",
    )(x, gain, w_in)


def _sb_attn_body(q_ref, k_ref, v_ref, o_ref, acc_ref, carry_ref, *, tq, tk, hps):
    qi = pl.program_id(2)
    diag_blocks = tq // tk
    row = lax.broadcasted_iota(jnp.int32, (tk, tk), 0)
    col = lax.broadcasted_iota(jnp.int32, (tk, tk), 1)
    suffix = (row > col).astype(BF16)
    q_pos = lax.broadcasted_iota(jnp.int32, (tq, tk), 0)
    k_pos = lax.broadcasted_iota(jnp.int32, (tq, tk), 1)

    def visit(kb, diag_index):
        start = pl.multiple_of(kb * tk, tk)
        heads = [slice(hd * HEAD_DIM, (hd + 1) * HEAD_DIM) for hd in range(hps)]
        if diag_index is not None:
            before = (k_pos + diag_index * tk) < q_pos
        zs, sps = [], []
        for sl in heads:
            z = _dot_nt(q_ref[:, sl], k_ref[pl.ds(start, tk), sl])
            neg_abs = lax.bitcast_convert_type(
                lax.bitcast_convert_type(z, jnp.uint32) | jnp.uint32(0x80000000), F32)
            sp = jnp.maximum(z, 0.0) + jnp.log2(1.0 + jnp.exp2(neg_abs))
            if diag_index is not None:
                sp = jnp.where(before, sp, 0.0)
            zs.append(z)
            sps.append(sp)
        spbs = [sp.astype(BF16) for sp in sps]
        inner_all = _dot(jnp.concatenate(spbs, axis=0), suffix)
        for hd, sl in enumerate(heads):
            inner = inner_all[hd * tq:(hd + 1) * tq]
            total = inner[:, 0:1] + spbs[hd][:, 0:1].astype(F32)
            carry = carry_ref[:, sl]
            carry_k = jnp.concatenate([carry] * (tk // HEAD_DIM), axis=1)
            w = jnp.exp2((zs[hd] - sps[hd]) - inner - carry_k)
            if diag_index is not None:
                w = jnp.where(before, w, 0.0)
            acc_ref[:, sl] += _dot(w.astype(BF16), v_ref[pl.ds(start, tk), sl])
            carry_ref[:, sl] = carry + jnp.broadcast_to(total, carry.shape)

    acc_ref[...] = jnp.zeros_like(acc_ref)
    carry_ref[...] = jnp.zeros_like(carry_ref)
    for j in reversed(range(diag_blocks)):
        visit(qi * diag_blocks + j, j)

    n_far = qi * diag_blocks

    def more(state):
        i, least_carry = state
        return jnp.logical_and(i < n_far, least_carry < F32_EXP2_UNDERFLOW)

    def step(state):
        i, _ = state
        visit(n_far - 1 - i, None)
        return i + 1, jnp.min(carry_ref[...])

    lax.while_loop(more, step, (jnp.int32(0), jnp.min(carry_ref[...])))
    o_ref[...] = acc_ref[...].astype(o_ref.dtype)


def _sb_attention(qkv, *, batch, seq, heads, tq, tk, hps):
    nq = seq // tq
    hg = heads // hps
    w = hps * HEAD_DIM
    body = functools.partial(_sb_attn_body, tq=tq, tk=tk, hps=hps)
    return pl.pallas_call(
        body,
        grid=(batch, hg, nq),
        in_specs=[
            pl.BlockSpec((tq, w), lambda b, h, i: (b * nq + i, h)),
            pl.BlockSpec((seq, w), lambda b, h, i: (b, hg + h)),
            pl.BlockSpec((seq, w), lambda b, h, i: (b, 2 * hg + h)),
        ],
        out_specs=pl.BlockSpec((tq, w), lambda b, h, i: (b * nq + i, h)),
        out_shape=jax.ShapeDtypeStruct((batch * seq, heads * HEAD_DIM), BF16),
        scratch_shapes=[pltpu.VMEM((tq, w), F32), pltpu.VMEM((tq, w), F32)],
        compiler_params=_cparams("parallel", "parallel", "arbitrary"),
        name="sb_attention",
    )(qkv, qkv, qkv)


def _mem_kv_body(mem_ref, gain_ref, w_ref, kgain_ref, k_ref, v_ref):
    h = _rms(mem_ref[...], gain_ref[...]).astype(BF16)
    kv = _dot(h, w_ref[...])
    for hd in range(MEM_HEADS):
        sl = slice(hd * HEAD_DIM, (hd + 1) * HEAD_DIM)
        k_ref[:, sl] = _rms(kv[:, sl], kgain_ref[...]).astype(BF16)
    v_ref[...] = kv[:, MEM_W:].astype(BF16)


def _mem_kv(mem, gain, w_kv, kgain, layer):
    rows, d = mem.shape
    return pl.pallas_call(
        _mem_kv_body,
        grid=(1,),
        in_specs=[
            pl.BlockSpec((rows, d), lambda i: (0, 0)),
            pl.BlockSpec((None, 1, d), lambda i: (layer, 0, 0)),
            pl.BlockSpec((None, d, 2 * MEM_W), lambda i: (layer, 0, 0)),
            pl.BlockSpec((None, 1, HEAD_DIM), lambda i: (layer, 0, 0)),
        ],
        out_specs=[pl.BlockSpec((rows, MEM_W), lambda i: (0, 0))] * 2,
        out_shape=[jax.ShapeDtypeStruct((rows, MEM_W), BF16)] * 2,
        compiler_params=_cparams("arbitrary"),
        name="mem_kv",
    )(mem, gain, w_kv, kgain)


def _cross_attention(qm, k, v, qgain):
    outs = []
    for hd in range(MEM_HEADS):
        sl = slice(hd * HEAD_DIM, (hd + 1) * HEAD_DIM)
        qn = _rms(qm[:, sl], qgain).astype(BF16)
        s = _dot_nt(qn, k[:, sl]) * (1.0 / math.sqrt(HEAD_DIM))
        p = jnp.exp(s - jnp.max(s, axis=-1, keepdims=True))
        p = p / jnp.sum(p, axis=-1, keepdims=True)
        outs.append(_dot(p.astype(BF16), v[:, sl]))
    return jnp.concatenate(outs, axis=1)


def _sb_outproj_body(x_ref, tok_ref, qm_ref, k_ref, v_ref, qgain_ref, w1_ref, w2_ref, o_ref):
    cross = _cross_attention(qm_ref[...].astype(F32), k_ref[...], v_ref[...], qgain_ref[...])
    o_ref[...] = (x_ref[...] + _dot(tok_ref[...], w1_ref[...])
                  + _dot(cross.astype(BF16), w2_ref[...]))


def _sb_outproj(x, tok, qkv, kmem, vmem, qgain, w_out, layer, j_layer, *, tm, seq, tok_w):
    m, d = x.shape
    n_mem = kmem.shape[0] // (m // seq)
    per_batch = seq // tm
    qm_block = tok_w * 3 // MEM_W
    return pl.pallas_call(
        _sb_outproj_body,
        grid=(m // tm,),
        in_specs=[
            pl.BlockSpec((tm, d), lambda i: (i, 0)),
            pl.BlockSpec((tm, tok_w), lambda i: (i, 0)),
            pl.BlockSpec((tm, MEM_W), lambda i: (i, qm_block)),
            pl.BlockSpec((n_mem, MEM_W), lambda i: (i // per_batch, 0)),
            pl.BlockSpec((n_mem, MEM_W), lambda i: (i // per_batch, 0)),
            pl.BlockSpec((None, 1, HEAD_DIM), lambda i: (layer, 0, 0)),
            _resident((None, tok_w, d), lambda i: (j_layer, 0, 0)),
            _resident((None, MEM_W, d), lambda i: (j_layer, tok_w // MEM_W, 0)),
        ],
        out_specs=pl.BlockSpec((tm, d), lambda i: (i, 0)),
        out_shape=jax.ShapeDtypeStruct((m, d), F32),
        compiler_params=_cparams("parallel"),
        name="sb_outproj",
    )(x, tok, qkv, kmem, vmem, qgain, w_out, w_out)


def _s5_inproj_body(x_ref, gain_ref, wut_ref, wq_ref, u_ref, qm_ref):
    h = _rms(x_ref[...], gain_ref[...]).astype(BF16)
    ut = _dot_nt(wut_ref[...], h)
    u_ref[...] = ut.reshape(u_ref.shape)
    qm_ref[0] = _dot(h, wq_ref[...]).astype(qm_ref.dtype)


def _s5_inproj(xc, gain, wut, wq, layer, j_layer):
    t, n, d = xc.shape
    tok_w = wut.shape[1]
    groups = tok_w // S5_GROUP
    return pl.pallas_call(
        _s5_inproj_body,
        grid=(t,),
        in_specs=[
            pl.BlockSpec((None, n, d), lambda i: (i, 0, 0)),
            pl.BlockSpec((None, 1, d), lambda i: (layer, 0, 0)),
            _resident((None, tok_w, d), lambda i: (j_layer, 0, 0)),
            _resident((None, d, MEM_W), lambda i: (j_layer, 0, 0)),
        ],
        out_specs=[
            pl.BlockSpec((groups, 1, S5_GROUP, n), lambda i: (0, i, 0, 0)),
            pl.BlockSpec((1, n, MEM_W), lambda i: (i, 0, 0)),
        ],
        out_shape=[
            jax.ShapeDtypeStruct((groups, t, S5_GROUP, n), F32),
            jax.ShapeDtypeStruct((t, n, MEM_W), BF16),
        ],
        compiler_params=_cparams("parallel"),
        name="s5_inproj",
    )(xc, gain, wut, wq)


def _cexp(zr, zi, k):
    mag = jnp.exp(zr * k)
    return mag * jnp.cos(zi * k), mag * jnp.sin(zi * k)


def _s5_core_body(u_ref, ldt_ref, arc_ref, aic_ref, br_ref, bi_ref,
                  cr_ref, ci_ref, d_ref, exp_ref, tile_ref, y_ref, mt_ref, q_ref, *, n_chunks):
    for g in range(u_ref.shape[0]):
        _s5_core_group(g, u_ref, ldt_ref, arc_ref, aic_ref, br_ref, bi_ref, cr_ref, ci_ref,
                       d_ref, exp_ref, tile_ref, y_ref, mt_ref, q_ref, n_chunks=n_chunks)


def _s5_core_group(g, u_ref, ldt_ref, arc_ref, aic_ref, br_ref, bi_ref,
                   cr_ref, ci_ref, d_ref, exp_ref, tile_ref, y_ref, mt_ref, q_ref, *, n_chunks):
    t = S5_CHUNK
    c = S5_GROUP
    ns = S5_STATE
    n = u_ref.shape[-1]
    dt = jnp.exp(ldt_ref[g])

    lr, li = arc_ref[g], aic_ref[g]
    zr, zi = lr * dt, li * dt
    back = (t - 1 - lax.broadcasted_iota(jnp.int32, (ns, t), 1)).astype(F32)
    pwr, pwi = _cexp(zr, zi, back)
    abr, abi = pwr[:, t - 2:t - 1], pwi[:, t - 2:t - 1]
    top_r, top_i = pwr[:, 0:1], pwi[:, 0:1]
    mr, mi = top_r * abr - top_i * abi, top_r * abi + top_i * abr
    pwr_t, pwi_t = pwr.T, pwi.T

    nr, ni = abr - 1.0, abi
    den = lr * lr + li * li
    fr = (nr * lr + ni * li) / den
    fi = (ni * lr - nr * li) / den
    b_re, b_im = br_ref[g], bi_ref[g]
    bbr = fr * b_re - fi * b_im
    bbi = fr * b_im + fi * b_re

    pwr_x, pwi_x = _dot_sel(pwr, exp_ref[...]), _dot_sel(pwi, exp_ref[...])
    bbr_x, bbi_x = _dot_sel(bbr, tile_ref[...]), _dot_sel(bbi, tile_ref[...])
    p_re = pwr_x * bbr_x - pwi_x * bbi_x
    p_im = pwr_x * bbi_x + pwi_x * bbr_x
    p = jnp.concatenate([p_re, p_im], axis=0)

    c_re, c_im = cr_ref[g], ci_ref[g]
    c_ext = jnp.concatenate([c_re, -c_im], axis=1)
    k_rev = jnp.concatenate([_dot3(c_ext, p), jnp.zeros((c, t * c), F32)], axis=1)
    width = 2 * t * c
    for step in range(t):
        off = (t - 1 - step) * c
        win = k_rev if off == 0 else pltpu.roll(k_rev, width - off, 1)
        mt_ref[g, step * c:(step + 1) * c, :] = win[:, :t * c].astype(BF16)

    for step in range(t):
        if step < t - 1:
            src = t - 2 - step
            ar, ai = pwr_t[src:src + 1, :], pwi_t[src:src + 1, :]
        else:
            hr, hi, lr_, li_ = pwr_t[0:1], pwi_t[0:1], pwr_t[t - 2:t - 1], pwi_t[t - 2:t - 1]
            ar, ai = hr * lr_ - hi * li_, hr * li_ + hi * lr_
        q_ref[g, step * c:(step + 1) * c, 0:ns] = (c_re * ar - c_im * ai).astype(BF16)
        q_ref[g, step * c:(step + 1) * c, ns:2 * ns] = (-(c_re * ai + c_im * ar)).astype(BF16)

    u = u_ref[g].reshape(t * c, n)
    ub = u.astype(BF16)
    inj = _dot(p.astype(BF16), ub)
    s_re, s_im = inj[:ns], inj[ns:]

    pos = lax.broadcasted_iota(jnp.int32, (ns, n), 1) % n_chunks
    shift = 1
    while shift < n_chunks:
        keep = pos >= shift
        sh_re = jnp.where(keep, pltpu.roll(s_re, shift, 1), 0.0)
        sh_im = jnp.where(keep, pltpu.roll(s_im, shift, 1), 0.0)
        s_re, s_im = s_re + mr * sh_re - mi * sh_im, s_im + mr * sh_im + mi * sh_re
        mr, mi = mr * mr - mi * mi, 2.0 * mr * mi
        shift *= 2
    first = pos >= 1
    prev = jnp.concatenate([jnp.where(first, pltpu.roll(s_re, 1, 1), 0.0),
                            jnp.where(first, pltpu.roll(s_im, 1, 1), 0.0)], axis=0)

    dcol = jnp.concatenate([d_ref[g]] * t, axis=0)
    y = _dot(mt_ref[g], ub) + _dot(q_ref[g], prev.astype(BF16)) + dcol * u
    y_ref[g] = jax.nn.gelu(y).reshape(t, c, n)


def _s5_core(u4, ldt, arc, aic, b_re, b_im, c_re, c_im, dcol, expand, tile, j_layer, *, n_chunks,
             gps):
    groups, t, c, n = u4.shape
    ns = S5_STATE
    body = functools.partial(_s5_core_body, n_chunks=n_chunks)

    def per_group(shape):
        return pl.BlockSpec((None, gps) + shape, lambda g: (j_layer, g, 0, 0))

    return pl.pallas_call(
        body,
        grid=(groups // gps,),
        in_specs=[
            pl.BlockSpec((gps, t, c, n), lambda g: (g, 0, 0, 0)),
            per_group((1, 1)),
            per_group((ns, 1)), per_group((ns, 1)),
            per_group((ns, c)), per_group((ns, c)),
            per_group((c, ns)), per_group((c, ns)),
            per_group((c, 1)),
            pl.BlockSpec((t, t * c), lambda g: (0, 0)),
            pl.BlockSpec((c, t * c), lambda g: (0, 0)),
        ],
        out_specs=pl.BlockSpec((gps, t, c, n), lambda g: (g, 0, 0, 0)),
        out_shape=jax.ShapeDtypeStruct((groups, t, c, n), F32),
        scratch_shapes=[pltpu.VMEM((gps, t * c, t * c), BF16),
                        pltpu.VMEM((gps, t * c, 2 * ns), BF16)],
        compiler_params=_cparams("parallel"),
        name="s5_core",
    )(u4, ldt, arc, aic, b_re, b_im, c_re, c_im, dcol, expand, tile)


def _s5_outproj_body(x_ref, y_ref, qm_ref, k_ref, v_ref, qgain_ref, wglut_ref, woutt_ref, o_ref,
                     *, n_mem, nc):
    steps, n, d = x_ref.shape
    b = n // nc
    groups, _, c, _ = y_ref.shape
    for s in range(steps):
        yt = y_ref[:, s].reshape(groups * c, n)
        gate = _dot(wglut_ref[...], yt.astype(BF16))
        tok_t = (yt * jax.nn.sigmoid(gate)).astype(BF16)
        qm = qm_ref[s].astype(F32)
        cross = jnp.concatenate(
            [_cross_attention(qm[i * nc:(i + 1) * nc], k_ref[i * n_mem:(i + 1) * n_mem],
                              v_ref[i * n_mem:(i + 1) * n_mem], qgain_ref[...])
             for i in range(b)], axis=0)
        mixed_t = jnp.concatenate([tok_t, cross.T.astype(BF16)], axis=0)
        upd = _dot(woutt_ref[...], mixed_t).T
        o_ref[s] = x_ref[s] + upd


def _s5_outproj(xc, y4, qm, kmem, vmem, qgain, wglut, woutt, layer, j_layer, *, n_chunks, steps):
    t, n, d = xc.shape
    groups, _, c, _ = y4.shape
    tok_w = groups * c
    b = n // n_chunks
    n_mem = kmem.shape[0] // b
    body = functools.partial(_s5_outproj_body, n_mem=n_mem, nc=n_chunks)
    return pl.pallas_call(
        body,
        grid=(t // steps,),
        in_specs=[
            pl.BlockSpec((steps, n, d), lambda i: (i, 0, 0)),
            pl.BlockSpec((groups, steps, c, n), lambda i: (0, i, 0, 0)),
            pl.BlockSpec((steps, n, MEM_W), lambda i: (i, 0, 0)),
            pl.BlockSpec((b * n_mem, MEM_W), lambda i: (0, 0)),
            pl.BlockSpec((b * n_mem, MEM_W), lambda i: (0, 0)),
            pl.BlockSpec((None, 1, HEAD_DIM), lambda i: (layer, 0, 0)),
            _resident((None, tok_w, tok_w), lambda i: (j_layer, 0, 0)),
            _resident((None, d, d), lambda i: (j_layer, 0, 0)),
        ],
        out_specs=pl.BlockSpec((steps, n, d), lambda i: (i, 0, 0)),
        out_shape=jax.ShapeDtypeStruct(xc.shape, F32),
        compiler_params=_cparams("parallel"),
        name="s5_outproj",
    )(xc, y4, qm, kmem, vmem, qgain, wglut, woutt)


FFN_TM = 1024
FFN_TF = 512
PROJ_TM = 2048
PROJ_TN = 512
OUT_TM = 512
S5_OUT_STEPS = 2
S5_GROUPS_PER_STEP = 2
SB_TQ = 512
SB_TK = 256
SB_HEADS_PER_STEP = 4


def kernel(x, mem, ffn1_norm, ffn1_w_gu, ffn1_w_down, mix_norm, mem_norm, w_mem_kv, xq_norm,
           xk_norm, w_out, ffn2_norm, ffn2_w_gu, ffn2_w_down, sb_w_in, s5_w_in, s5_log_dt,
           s5_a_re, s5_a_im, s5_b_re, s5_b_im, s5_c_re, s5_c_im, s5_d, s5_w_glu):
    batch, seq, d = x.shape
    depth = ffn1_norm.shape[0]
    n_mem = mem.shape[1]
    tok_w = d - MEM_W
    heads = tok_w // HEAD_DIM
    groups = tok_w // S5_GROUP
    n_chunks = seq // S5_CHUNK
    n_b = s5_w_in.shape[0]
    ns, c, t = S5_STATE, S5_GROUP, S5_CHUNK

    def row(g):
        return g.reshape(g.shape[0], 1, g.shape[1])

    ffn1_norm, ffn2_norm, mix_norm, mem_norm = map(row, (ffn1_norm, ffn2_norm, mix_norm, mem_norm))
    xq_norm, xk_norm = row(xq_norm), row(xk_norm)
    w_kv = w_mem_kv.astype(BF16)
    w_out_b = w_out[0::2].astype(BF16)
    w_out_t = jnp.swapaxes(w_out[1::2], 1, 2).astype(BF16)
    sb_w = sb_w_in.astype(BF16)
    s5_wu_t = jnp.swapaxes(s5_w_in[:, :, :tok_w], 1, 2).astype(BF16)
    s5_wq = s5_w_in[:, :, tok_w:].astype(BF16)
    s5_wglu_t = jnp.swapaxes(s5_w_glu, 1, 2).astype(BF16)

    ldt = s5_log_dt.reshape(n_b, groups, 1, 1)
    arc, aic = s5_a_re.reshape(n_b, groups, ns, 1), s5_a_im.reshape(n_b, groups, ns, 1)
    dcol = s5_d.reshape(n_b, groups, c, 1)
    lane = jnp.arange(t * c)
    expand = (lane[None, :] // c == jnp.arange(t)[:, None]).astype(BF16)
    tile = (lane[None, :] % c == jnp.arange(c)[:, None]).astype(BF16)

    mem2 = mem.reshape(batch * n_mem, d)
    xs = x.reshape(batch * seq, d)
    def ffn(xin, gain, w_gu, w_down, layer, layout="rows", cast_next=None):
        tm = min(FFN_TM, batch * seq)
        return _ffn(xin, gain, w_gu, w_down, layer, tm=tm, tf=FFN_TF, layout=layout, chunk=t,
                    cast_next=cast_next)

    w_ffn = (ffn1_w_gu[0].astype(BF16), ffn1_w_down[0].astype(BF16))
    for layer in range(depth):
        kmem, vmem = _mem_kv(mem2, mem_norm, w_kv, xk_norm, layer)
        j_layer = layer // 2
        after_ffn1 = (ffn2_w_gu, ffn2_w_down, layer)
        after_ffn2 = (ffn1_w_gu, ffn1_w_down, layer + 1) if layer + 1 < depth else None
        if layer % 2 == 0:
            xs, w_ffn = ffn(xs, ffn1_norm, *w_ffn, layer, cast_next=after_ffn1)
            qkv = _sb_inproj(xs, mix_norm, sb_w, layer, j_layer,
                             tm=min(PROJ_TM, batch * seq), tn=PROJ_TN,
                             tok_w=tok_w)
            tok = _sb_attention(qkv, batch=batch, seq=seq, heads=heads, tq=SB_TQ, tk=SB_TK,
                                hps=SB_HEADS_PER_STEP)
            xs = _sb_outproj(xs, tok, qkv, kmem, vmem, xq_norm, w_out_b, layer, j_layer,
                             tm=OUT_TM, seq=seq, tok_w=tok_w)
            xs, w_ffn = ffn(xs, ffn2_norm, *w_ffn, layer, cast_next=after_ffn2)
        else:
            xc, w_ffn = ffn(xs, ffn1_norm, *w_ffn, layer, layout="to_chunked",
                            cast_next=after_ffn1)
            u4, qm = _s5_inproj(xc, mix_norm, s5_wu_t, s5_wq, layer, j_layer)
            y4 = _s5_core(u4, ldt, arc, aic, s5_b_re, s5_b_im, s5_c_re, s5_c_im, dcol,
                          expand, tile, j_layer, n_chunks=n_chunks, gps=S5_GROUPS_PER_STEP)
            xc = _s5_outproj(xc, y4, qm, kmem, vmem, xq_norm, s5_wglu_t, w_out_t, layer, j_layer,
                             n_chunks=n_chunks, steps=S5_OUT_STEPS)
            xs, w_ffn = ffn(xc, ffn2_norm, *w_ffn, layer, layout="from_chunked",
                            cast_next=after_ffn2)
    return xs.reshape(batch, seq, d)
```

```python
import functools
import math

import jax
import jax.numpy as jnp
from jax import lax
from jax.experimental import pallas as pl
from jax.experimental.pallas import tpu as pltpu

F32 = jnp.float32
BF16 = jnp.bfloat16

EPS = 1e-6
HEAD_DIM = 128
MEM_HEADS = 4
MEM_W = MEM_HEADS * HEAD_DIM
S5_GROUP = 16
S5_STATE = 64
S5_CHUNK = 64
SWAP_SLAB = 256
F32_EXP2_UNDERFLOW = 151.0

V7X_VMEM_LIMIT_BYTES = 62 * 1024 * 1024


def _cparams(*sem):
    return pltpu.CompilerParams(dimension_semantics=sem, vmem_limit_bytes=V7X_VMEM_LIMIT_BYTES)


def _resident(block_shape, index_map):
    return pl.BlockSpec(block_shape, index_map, pipeline_mode=pl.Buffered(1))


def _rms(x, gain):
    return x * lax.rsqrt(jnp.mean(x * x, axis=-1, keepdims=True) + EPS) * gain


def _dot(a, b):
    return jnp.dot(a, b, preferred_element_type=F32)


def _dot_nt(a, b):
    return lax.dot_general(a, b, (((1,), (1,)), ((), ())), preferred_element_type=F32)


def _split3(x):
    h = x.astype(BF16)
    r = x - h.astype(F32)
    m = r.astype(BF16)
    l = (r - m.astype(F32)).astype(BF16)
    return h, m, l


def _dot_sel(x, sel):
    h, m, l = _split3(x)
    return _dot(h, sel) + _dot(m, sel) + _dot(l, sel)


def _dot3(a, b):
    ah = a.astype(BF16)
    al = (a - ah.astype(F32)).astype(BF16)
    bh = b.astype(BF16)
    bl = (b - bh.astype(F32)).astype(BF16)
    return _dot(ah, bh) + _dot(ah, bl) + _dot(al, bh)


def _ffn_body(*refs, layout, chunk, cast_next):
    n_extra = 2 if cast_next else 0
    x_ref, gain_ref, wg_ref, wu_ref, wd_ref = refs[:5]
    next_in = refs[5:5 + n_extra]
    o_ref = refs[5 + n_extra]
    next_out = refs[6 + n_extra:6 + 2 * n_extra]
    h_ref = refs[6 + 2 * n_extra]

    j = pl.program_id(1)
    tm, d = h_ref.shape
    t = chunk
    cpb = tm // chunk
    slabs = [slice(s, s + SWAP_SLAB) for s in range(0, d, SWAP_SLAB)]

    @pl.when(j == 0)
    def _():
        x = x_ref[...].reshape(tm, d)
        o_ref[...] = x.reshape(o_ref.shape)
        h_ref[...] = _rms(x, gain_ref[...]).astype(BF16)

    h = h_ref[...]
    g = _dot(h, wg_ref[...])
    u = _dot(h, wu_ref[...])
    a = (g * jax.nn.sigmoid(g) * (u * 0.5)).astype(BF16)
    o_ref[...] += _dot(a, wd_ref[...]).reshape(o_ref.shape)

    for src, dst in zip(next_in, next_out):
        dst[...] = src[...].astype(BF16)

    if layout != "rows":
        @pl.when(j == pl.num_programs(1) - 1)
        def _():
            for sl in slabs:
                if layout == "to_chunked":
                    rows = o_ref[:, :, sl].reshape(cpb, t, SWAP_SLAB)
                    o_ref[:, :, sl] = jnp.swapaxes(rows, 0, 1)
                else:
                    rows = o_ref[:, sl].reshape(t, cpb, SWAP_SLAB)
                    o_ref[:, sl] = jnp.swapaxes(rows, 0, 1).reshape(tm, SWAP_SLAB)


def _ffn(x, gain, w_gu, w_down, layer, *, tm, tf, layout="rows", chunk=None, cast_next=None):
    d = x.shape[-1]
    m = x.size // d
    f = w_down.shape[0]
    nf = f // tf
    n_i = m // tm
    rows_spec = pl.BlockSpec((tm, d), lambda i, j: (i, 0))
    rows_shape = jax.ShapeDtypeStruct((m, d), F32)
    scratch = [pltpu.VMEM((tm, d), BF16)]
    if layout == "rows":
        x_spec, o_spec, o_shape = rows_spec, rows_spec, rows_shape
    else:
        cpb = tm // chunk
        chunk_spec = pl.BlockSpec((chunk, cpb, d), lambda i, j: (0, i, 0))
        chunk_shape = jax.ShapeDtypeStruct((chunk, m // chunk, d), F32)
        if layout == "to_chunked":
            x_spec, o_spec, o_shape = rows_spec, chunk_spec, chunk_shape
        else:
            x_spec, o_spec, o_shape = chunk_spec, rows_spec, rows_shape
    in_specs = [
        x_spec,
        pl.BlockSpec((None, 1, d), lambda i, j: (layer, 0, 0)),
        pl.BlockSpec((d, tf), lambda i, j: (0, j)),
        pl.BlockSpec((d, tf), lambda i, j: (0, nf + j)),
        pl.BlockSpec((tf, d), lambda i, j: (j, 0)),
    ]
    args = [x, gain, w_gu, w_gu, w_down]
    out_specs, out_shapes = [o_spec], [o_shape]
    if cast_next is not None:
        gu_f32, down_f32, nxt = cast_next
        gu_rows, gu_cols, down_rows = d // n_i, 2 * f // nf, f // (n_i * nf)
        assert gu_rows * n_i == d and gu_cols * nf == 2 * f and down_rows * n_i * nf == f
        in_specs += [pl.BlockSpec((None, gu_rows, gu_cols), lambda i, j: (nxt, i, j)),
                     pl.BlockSpec((None, down_rows, d), lambda i, j: (nxt, i * nf + j, 0))]
        args += [gu_f32, down_f32]
        out_specs += [pl.BlockSpec((gu_rows, gu_cols), lambda i, j: (i, j)),
                      pl.BlockSpec((down_rows, d), lambda i, j: (i * nf + j, 0))]
        out_shapes += [jax.ShapeDtypeStruct((d, 2 * f), BF16), jax.ShapeDtypeStruct((f, d), BF16)]
    outs = pl.pallas_call(
        functools.partial(_ffn_body, layout=layout, chunk=chunk, cast_next=cast_next is not None),
        grid=(n_i, nf),
        in_specs=in_specs,
        out_specs=out_specs,
        out_shape=out_shapes,
        scratch_shapes=scratch,
        compiler_params=_cparams("parallel", "arbitrary"),
        name="ffn_" + layout,
    )(*args)
    return outs[0], tuple(outs[1:])


def _sb_inproj_body(x_ref, gain_ref, w_ref, o_ref, h_ref, *, n_q_blocks, q_scale):
    j = pl.program_id(1)

    @pl.when(j == 0)
    def _():
        h_ref[...] = _rms(x_ref[...], gain_ref[...]).astype(BF16)

    acc = _dot(h_ref[...], w_ref[...])
    scale = jnp.where(j < n_q_blocks, q_scale, 1.0).astype(F32)
    o_ref[...] = (acc * scale).astype(o_ref.dtype)


def _sb_inproj(x, gain, w_in, layer, j_layer, *, tm, tn, tok_w):
    m, d = x.shape
    n = w_in.shape[2]
    body = functools.partial(_sb_inproj_body, n_q_blocks=tok_w // tn,
                             q_scale=math.log2(math.e) / math.sqrt(HEAD_DIM))
    return pl.pallas_call(
        body,
        grid=(m // tm, n // tn),
        in_specs=[
            pl.BlockSpec((tm, d), lambda i, j: (i, 0)),
            pl.BlockSpec((None, 1, d), lambda i, j: (layer, 0, 0)),
            pl.BlockSpec((None, d, tn), lambda i, j: (j_layer, 0, j)),
        ],
        out_specs=pl.BlockSpec((tm, tn), lambda i, j: (i, j)),
        out_shape=jax.ShapeDtypeStruct((m, n), BF16),
        scratch_shapes=[pltpu.VMEM((tm, d), BF16)],
        compiler_params=_cparams("parallel", "arbitrary"),
        name="sb_inproj",
    )(x, gain, w_in)


def _sb_attn_body(q_ref, k_ref, v_ref, o_ref, acc_ref, carry_ref, *, tq, tk, hps):
    qi = pl.program_id(2)
    diag_blocks = tq // tk
    row = lax.broadcasted_iota(jnp.int32, (tk, tk), 0)
    col = lax.broadcasted_iota(jnp.int32, (tk, tk), 1)
    suffix = (row > col).astype(BF16)

    def visit(kb, diag_index):
        start = pl.multiple_of(kb * tk, tk)
        heads = [slice(hd * HEAD_DIM, (hd + 1) * HEAD_DIM) for hd in range(hps)]
        row0 = 0 if diag_index is None else diag_index * tk
        rows = slice(row0, tq)
        n_rows = tq - row0
        if diag_index is not None:
            before = (lax.broadcasted_iota(jnp.int32, (n_rows, tk), 1)
                      < lax.broadcasted_iota(jnp.int32, (n_rows, tk), 0))
        zs, sps = [], []
        for sl in heads:
            z = _dot_nt(q_ref[rows, sl], k_ref[pl.ds(start, tk), sl])
            neg_abs = lax.bitcast_convert_type(
                lax.bitcast_convert_type(z, jnp.uint32) | jnp.uint32(0x80000000), F32)
            sp = jnp.maximum(z, 0.0) + jnp.log2(1.0 + jnp.exp2(neg_abs))
            if diag_index is not None:
                sp = jnp.where(before, sp, 0.0)
            zs.append(z)
            sps.append(sp)
        spbs = [sp.astype(BF16) for sp in sps]
        inner_all = _dot(jnp.concatenate(spbs, axis=0), suffix)
        for hd, sl in enumerate(heads):
            inner = inner_all[hd * n_rows:(hd + 1) * n_rows]
            total = inner[:, 0:1] + spbs[hd][:, 0:1].astype(F32)
            carry = carry_ref[rows, sl]
            carry_k = jnp.concatenate([carry] * (tk // HEAD_DIM), axis=1)
            w = jnp.exp2((zs[hd] - sps[hd]) - inner - carry_k)
            if diag_index is not None:
                w = jnp.where(before, w, 0.0)
            acc_ref[rows, sl] += _dot(w.astype(BF16), v_ref[pl.ds(start, tk), sl])
            carry_ref[rows, sl] = carry + jnp.broadcast_to(total, carry.shape)

    acc_ref[...] = jnp.zeros_like(acc_ref)
    carry_ref[...] = jnp.zeros_like(carry_ref)
    for j in reversed(range(diag_blocks)):
        visit(qi * diag_blocks + j, j)

    n_far = qi * diag_blocks

    def more(state):
        i, least_carry = state
        return jnp.logical_and(i < n_far, least_carry < F32_EXP2_UNDERFLOW)

    def step(state):
        i, _ = state
        visit(n_far - 1 - i, None)
        return i + 1, jnp.min(carry_ref[...])

    lax.while_loop(more, step, (jnp.int32(0), jnp.min(carry_ref[...])))
    o_ref[...] = acc_ref[...].astype(o_ref.dtype)


def _sb_attention(qkv, *, batch, seq, heads, tq, tk, hps):
    nq = seq // tq
    hg = heads // hps
    w = hps * HEAD_DIM
    body = functools.partial(_sb_attn_body, tq=tq, tk=tk, hps=hps)
    return pl.pallas_call(
        body,
        grid=(batch, hg, nq),
        in_specs=[
            pl.BlockSpec((tq, w), lambda b, h, i: (b * nq + i, h)),
            pl.BlockSpec((seq, w), lambda b, h, i: (b, hg + h)),
            pl.BlockSpec((seq, w), lambda b, h, i: (b, 2 * hg + h)),
        ],
        out_specs=pl.BlockSpec((tq, w), lambda b, h, i: (b * nq + i, h)),
        out_shape=jax.ShapeDtypeStruct((batch * seq, heads * HEAD_DIM), BF16),
        scratch_shapes=[pltpu.VMEM((tq, w), F32), pltpu.VMEM((tq, w), F32)],
        compiler_params=_cparams("parallel", "parallel", "arbitrary"),
        name="sb_attention",
    )(qkv, qkv, qkv)


def _mem_kv_body(mem_ref, gain_ref, w_ref, kgain_ref, k_ref, v_ref):
    h = _rms(mem_ref[...], gain_ref[...]).astype(BF16)
    kv = _dot(h, w_ref[...])
    for hd in range(MEM_HEADS):
        sl = slice(hd * HEAD_DIM, (hd + 1) * HEAD_DIM)
        k_ref[:, sl] = _rms(kv[:, sl], kgain_ref[...]).astype(BF16)
    v_ref[...] = kv[:, MEM_W:].astype(BF16)


def _mem_kv(mem, gain, w_kv, kgain, layer):
    rows, d = mem.shape
    return pl.pallas_call(
        _mem_kv_body,
        grid=(1,),
        in_specs=[
            pl.BlockSpec((rows, d), lambda i: (0, 0)),
            pl.BlockSpec((None, 1, d), lambda i: (layer, 0, 0)),
            pl.BlockSpec((None, d, 2 * MEM_W), lambda i: (layer, 0, 0)),
            pl.BlockSpec((None, 1, HEAD_DIM), lambda i: (layer, 0, 0)),
        ],
        out_specs=[pl.BlockSpec((rows, MEM_W), lambda i: (0, 0))] * 2,
        out_shape=[jax.ShapeDtypeStruct((rows, MEM_W), BF16)] * 2,
        compiler_params=_cparams("arbitrary"),
        name="mem_kv",
    )(mem, gain, w_kv, kgain)


def _cross_attention(qm, k, v, qgain):
    outs = []
    for hd in range(MEM_HEADS):
        sl = slice(hd * HEAD_DIM, (hd + 1) * HEAD_DIM)
        qn = _rms(qm[:, sl], qgain).astype(BF16)
        s = _dot_nt(qn, k[:, sl]) * (1.0 / math.sqrt(HEAD_DIM))
        p = jnp.exp(s - jnp.max(s, axis=-1, keepdims=True))
        p = p / jnp.sum(p, axis=-1, keepdims=True)
        outs.append(_dot(p.astype(BF16), v[:, sl]))
    return jnp.concatenate(outs, axis=1)


def _sb_outproj_body(x_ref, tok_ref, qm_ref, k_ref, v_ref, qgain_ref, w1_ref, w2_ref, o_ref):
    cross = _cross_attention(qm_ref[...].astype(F32), k_ref[...], v_ref[...], qgain_ref[...])
    o_ref[...] = (x_ref[...] + _dot(tok_ref[...], w1_ref[...])
                  + _dot(cross.astype(BF16), w2_ref[...]))


def _sb_outproj(x, tok, qkv, kmem, vmem, qgain, w_out, layer, j_layer, *, tm, seq, tok_w):
    m, d = x.shape
    n_mem = kmem.shape[0] // (m // seq)
    per_batch = seq // tm
    qm_block = tok_w * 3 // MEM_W
    return pl.pallas_call(
        _sb_outproj_body,
        grid=(m // tm,),
        in_specs=[
            pl.BlockSpec((tm, d), lambda i: (i, 0)),
            pl.BlockSpec((tm, tok_w), lambda i: (i, 0)),
            pl.BlockSpec((tm, MEM_W), lambda i: (i, qm_block)),
            pl.BlockSpec((n_mem, MEM_W), lambda i: (i // per_batch, 0)),
            pl.BlockSpec((n_mem, MEM_W), lambda i: (i // per_batch, 0)),
            pl.BlockSpec((None, 1, HEAD_DIM), lambda i: (layer, 0, 0)),
            _resident((None, tok_w, d), lambda i: (j_layer, 0, 0)),
            _resident((None, MEM_W, d), lambda i: (j_layer, tok_w // MEM_W, 0)),
        ],
        out_specs=pl.BlockSpec((tm, d), lambda i: (i, 0)),
        out_shape=jax.ShapeDtypeStruct((m, d), F32),
        compiler_params=_cparams("parallel"),
        name="sb_outproj",
    )(x, tok, qkv, kmem, vmem, qgain, w_out, w_out)


def _s5_inproj_body(x_ref, gain_ref, wut_ref, wq_ref, u_ref, qm_ref):
    h = _rms(x_ref[...], gain_ref[...]).astype(BF16)
    ut = _dot_nt(wut_ref[...], h)
    u_ref[...] = ut.reshape(u_ref.shape)
    qm_ref[0] = _dot(h, wq_ref[...]).astype(qm_ref.dtype)


def _s5_inproj(xc, gain, wut, wq, layer, j_layer):
    t, n, d = xc.shape
    tok_w = wut.shape[1]
    groups = tok_w // S5_GROUP
    return pl.pallas_call(
        _s5_inproj_body,
        grid=(t,),
        in_specs=[
            pl.BlockSpec((None, n, d), lambda i: (i, 0, 0)),
            pl.BlockSpec((None, 1, d), lambda i: (layer, 0, 0)),
            _resident((None, tok_w, d), lambda i: (j_layer, 0, 0)),
            _resident((None, d, MEM_W), lambda i: (j_layer, 0, 0)),
        ],
        out_specs=[
            pl.BlockSpec((groups, 1, S5_GROUP, n), lambda i: (0, i, 0, 0)),
            pl.BlockSpec((1, n, MEM_W), lambda i: (i, 0, 0)),
        ],
        out_shape=[
            jax.ShapeDtypeStruct((groups, t, S5_GROUP, n), F32),
            jax.ShapeDtypeStruct((t, n, MEM_W), BF16),
        ],
        compiler_params=_cparams("parallel"),
        name="s5_inproj",
    )(xc, gain, wut, wq)


def _cexp(zr, zi, k):
    mag = jnp.exp(zr * k)
    return mag * jnp.cos(zi * k), mag * jnp.sin(zi * k)


def _s5_core_body(u_ref, ldt_ref, arc_ref, aic_ref, br_ref, bi_ref,
                  cr_ref, ci_ref, d_ref, exp_ref, tile_ref, y_ref, mt_ref, q_ref, *, n_chunks):
    for g in range(u_ref.shape[0]):
        _s5_core_group(g, u_ref, ldt_ref, arc_ref, aic_ref, br_ref, bi_ref, cr_ref, ci_ref,
                       d_ref, exp_ref, tile_ref, y_ref, mt_ref, q_ref, n_chunks=n_chunks)


def _s5_core_group(g, u_ref, ldt_ref, arc_ref, aic_ref, br_ref, bi_ref,
                   cr_ref, ci_ref, d_ref, exp_ref, tile_ref, y_ref, mt_ref, q_ref, *, n_chunks):
    t = S5_CHUNK
    c = S5_GROUP
    ns = S5_STATE
    n = u_ref.shape[-1]
    dt = jnp.exp(ldt_ref[g])

    lr, li = arc_ref[g], aic_ref[g]
    zr, zi = lr * dt, li * dt
    back = (t - 1 - lax.broadcasted_iota(jnp.int32, (ns, t), 1)).astype(F32)
    pwr, pwi = _cexp(zr, zi, back)
    abr, abi = pwr[:, t - 2:t - 1], pwi[:, t - 2:t - 1]
    top_r, top_i = pwr[:, 0:1], pwi[:, 0:1]
    mr, mi = top_r * abr - top_i * abi, top_r * abi + top_i * abr
    pwr_t, pwi_t = pwr.T, pwi.T

    nr, ni = abr - 1.0, abi
    den = lr * lr + li * li
    fr = (nr * lr + ni * li) / den
    fi = (ni * lr - nr * li) / den
    b_re, b_im = br_ref[g], bi_ref[g]
    bbr = fr * b_re - fi * b_im
    bbi = fr * b_im + fi * b_re

    pwr_x, pwi_x = _dot_sel(pwr, exp_ref[...]), _dot_sel(pwi, exp_ref[...])
    bbr_x, bbi_x = _dot_sel(bbr, tile_ref[...]), _dot_sel(bbi, tile_ref[...])
    p_re = pwr_x * bbr_x - pwi_x * bbi_x
    p_im = pwr_x * bbi_x + pwi_x * bbr_x
    p = jnp.concatenate([p_re, p_im], axis=0)

    c_re, c_im = cr_ref[g], ci_ref[g]
    c_ext = jnp.concatenate([c_re, -c_im], axis=1)
    k_rev = jnp.concatenate([_dot3(c_ext, p), jnp.zeros((c, t * c), F32)], axis=1)
    width = 2 * t * c
    for step in range(t):
        off = (t - 1 - step) * c
        win = k_rev if off == 0 else pltpu.roll(k_rev, width - off, 1)
        mt_ref[g, step * c:(step + 1) * c, :] = win[:, :t * c].astype(BF16)

    for step in range(t):
        if step < t - 1:
            src = t - 2 - step
            ar, ai = pwr_t[src:src + 1, :], pwi_t[src:src + 1, :]
        else:
            hr, hi, lr_, li_ = pwr_t[0:1], pwi_t[0:1], pwr_t[t - 2:t - 1], pwi_t[t - 2:t - 1]
            ar, ai = hr * lr_ - hi * li_, hr * li_ + hi * lr_
        q_ref[g, step * c:(step + 1) * c, 0:ns] = (c_re * ar - c_im * ai).astype(BF16)
        q_ref[g, step * c:(step + 1) * c, ns:2 * ns] = (-(c_re * ai + c_im * ar)).astype(BF16)

    u = u_ref[g].reshape(t * c, n)
    ub = u.astype(BF16)
    inj = _dot(p.astype(BF16), ub)
    s_re, s_im = inj[:ns], inj[ns:]

    pos = lax.broadcasted_iota(jnp.int32, (ns, n), 1) % n_chunks
    shift = 1
    while shift < n_chunks:
        keep = pos >= shift
        sh_re = jnp.where(keep, pltpu.roll(s_re, shift, 1), 0.0)
        sh_im = jnp.where(keep, pltpu.roll(s_im, shift, 1), 0.0)
        s_re, s_im = s_re + mr * sh_re - mi * sh_im, s_im + mr * sh_im + mi * sh_re
        mr, mi = mr * mr - mi * mi, 2.0 * mr * mi
        shift *= 2
    first = pos >= 1
    prev = jnp.concatenate([jnp.where(first, pltpu.roll(s_re, 1, 1), 0.0),
                            jnp.where(first, pltpu.roll(s_im, 1, 1), 0.0)], axis=0)

    dcol = jnp.concatenate([d_ref[g]] * t, axis=0)
    y = _dot(mt_ref[g], ub) + _dot(q_ref[g], prev.astype(BF16)) + dcol * u
    y_ref[g] = jax.nn.gelu(y).reshape(t, c, n)


def _s5_core(u4, ldt, arc, aic, b_re, b_im, c_re, c_im, dcol, expand, tile, j_layer, *, n_chunks,
             gps):
    groups, t, c, n = u4.shape
    ns = S5_STATE
    body = functools.partial(_s5_core_body, n_chunks=n_chunks)

    def per_group(shape):
        return pl.BlockSpec((None, gps) + shape, lambda g: (j_layer, g, 0, 0))

    return pl.pallas_call(
        body,
        grid=(groups // gps,),
        in_specs=[
            pl.BlockSpec((gps, t, c, n), lambda g: (g, 0, 0, 0)),
            per_group((1, 1)),
            per_group((ns, 1)), per_group((ns, 1)),
            per_group((ns, c)), per_group((ns, c)),
            per_group((c, ns)), per_group((c, ns)),
            per_group((c, 1)),
            pl.BlockSpec((t, t * c), lambda g: (0, 0)),
            pl.BlockSpec((c, t * c), lambda g: (0, 0)),
        ],
        out_specs=pl.BlockSpec((gps, t, c, n), lambda g: (g, 0, 0, 0)),
        out_shape=jax.ShapeDtypeStruct((groups, t, c, n), F32),
        scratch_shapes=[pltpu.VMEM((gps, t * c, t * c), BF16),
                        pltpu.VMEM((gps, t * c, 2 * ns), BF16)],
        compiler_params=_cparams("parallel"),
        name="s5_core",
    )(u4, ldt, arc, aic, b_re, b_im, c_re, c_im, dcol, expand, tile)


def _s5_outproj_body(x_ref, y_ref, qm_ref, k_ref, v_ref, qgain_ref, wglut_ref, woutt_ref, o_ref,
                     *, n_mem, nc):
    steps, n, d = x_ref.shape
    b = n // nc
    groups, _, c, _ = y_ref.shape
    for s in range(steps):
        yt = y_ref[:, s].reshape(groups * c, n)
        gate = _dot(wglut_ref[...], yt.astype(BF16))
        tok_t = (yt * jax.nn.sigmoid(gate)).astype(BF16)
        qm = qm_ref[s].astype(F32)
        cross = jnp.concatenate(
            [_cross_attention(qm[i * nc:(i + 1) * nc], k_ref[i * n_mem:(i + 1) * n_mem],
                              v_ref[i * n_mem:(i + 1) * n_mem], qgain_ref[...])
             for i in range(b)], axis=0)
        mixed_t = jnp.concatenate([tok_t, cross.T.astype(BF16)], axis=0)
        upd = _dot(woutt_ref[...], mixed_t).T
        o_ref[s] = x_ref[s] + upd


def _s5_outproj(xc, y4, qm, kmem, vmem, qgain, wglut, woutt, layer, j_layer, *, n_chunks, steps):
    t, n, d = xc.shape
    groups, _, c, _ = y4.shape
    tok_w = groups * c
    b = n // n_chunks
    n_mem = kmem.shape[0] // b
    body = functools.partial(_s5_outproj_body, n_mem=n_mem, nc=n_chunks)
    return pl.pallas_call(
        body,
        grid=(t // steps,),
        in_specs=[
            pl.BlockSpec((steps, n, d), lambda i: (i, 0, 0)),
            pl.BlockSpec((groups, steps, c, n), lambda i: (0, i, 0, 0)),
            pl.BlockSpec((steps, n, MEM_W), lambda i: (i, 0, 0)),
            pl.BlockSpec((b * n_mem, MEM_W), lambda i: (0, 0)),
            pl.BlockSpec((b * n_mem, MEM_W), lambda i: (0, 0)),
            pl.BlockSpec((None, 1, HEAD_DIM), lambda i: (layer, 0, 0)),
            _resident((None, tok_w, tok_w), lambda i: (j_layer, 0, 0)),
            _resident((None, d, d), lambda i: (j_layer, 0, 0)),
        ],
        out_specs=pl.BlockSpec((steps, n, d), lambda i: (i, 0, 0)),
        out_shape=jax.ShapeDtypeStruct(xc.shape, F32),
        compiler_params=_cparams("parallel"),
        name="s5_outproj",
    )(xc, y4, qm, kmem, vmem, qgain, wglut, woutt)


FFN_TM = 1024
FFN_TF = 512
PROJ_TM = 2048
PROJ_TN = 512
OUT_TM = 512
S5_OUT_STEPS = 2
S5_GROUPS_PER_STEP = 2
SB_TQ = 512
SB_TK = 256
SB_HEADS_PER_STEP = 4


def kernel(x, mem, ffn1_norm, ffn1_w_gu, ffn1_w_down, mix_norm, mem_norm, w_mem_kv, xq_norm,
           xk_norm, w_out, ffn2_norm, ffn2_w_gu, ffn2_w_down, sb_w_in, s5_w_in, s5_log_dt,
           s5_a_re, s5_a_im, s5_b_re, s5_b_im, s5_c_re, s5_c_im, s5_d, s5_w_glu):
    batch, seq, d = x.shape
    depth = ffn1_norm.shape[0]
    n_mem = mem.shape[1]
    tok_w = d - MEM_W
    heads = tok_w // HEAD_DIM
    groups = tok_w // S5_GROUP
    n_chunks = seq // S5_CHUNK
    n_b = s5_w_in.shape[0]
    ns, c, t = S5_STATE, S5_GROUP, S5_CHUNK

    def row(g):
        return g.reshape(g.shape[0], 1, g.shape[1])

    ffn1_norm, ffn2_norm, mix_norm, mem_norm = map(row, (ffn1_norm, ffn2_norm, mix_norm, mem_norm))
    xq_norm, xk_norm = row(xq_norm), row(xk_norm)
    w_kv = w_mem_kv.astype(BF16)
    w_out_b = w_out[0::2].astype(BF16)
    w_out_t = jnp.swapaxes(w_out[1::2], 1, 2).astype(BF16)
    sb_w = sb_w_in.astype(BF16)
    s5_wu_t = jnp.swapaxes(s5_w_in[:, :, :tok_w], 1, 2).astype(BF16)
    s5_wq = s5_w_in[:, :, tok_w:].astype(BF16)
    s5_wglu_t = jnp.swapaxes(s5_w_glu, 1, 2).astype(BF16)

    ldt = s5_log_dt.reshape(n_b, groups, 1, 1)
    arc, aic = s5_a_re.reshape(n_b, groups, ns, 1), s5_a_im.reshape(n_b, groups, ns, 1)
    dcol = s5_d.reshape(n_b, groups, c, 1)
    lane = jnp.arange(t * c)
    expand = (lane[None, :] // c == jnp.arange(t)[:, None]).astype(BF16)
    tile = (lane[None, :] % c == jnp.arange(c)[:, None]).astype(BF16)

    mem2 = mem.reshape(batch * n_mem, d)
    xs = x.reshape(batch * seq, d)
    def ffn(xin, gain, w_gu, w_down, layer, layout="rows", cast_next=None):
        tm = min(FFN_TM, batch * seq)
        return _ffn(xin, gain, w_gu, w_down, layer, tm=tm, tf=FFN_TF, layout=layout, chunk=t,
                    cast_next=cast_next)

    w_ffn = (ffn1_w_gu[0].astype(BF16), ffn1_w_down[0].astype(BF16))
    for layer in range(depth):
        kmem, vmem = _mem_kv(mem2, mem_norm, w_kv, xk_norm, layer)
        j_layer = layer // 2
        after_ffn1 = (ffn2_w_gu, ffn2_w_down, layer)
        after_ffn2 = (ffn1_w_gu, ffn1_w_down, layer + 1) if layer + 1 < depth else None
        if layer % 2 == 0:
            xs, w_ffn = ffn(xs, ffn1_norm, *w_ffn, layer, cast_next=after_ffn1)
            qkv = _sb_inproj(xs, mix_norm, sb_w, layer, j_layer,
                             tm=min(PROJ_TM, batch * seq), tn=PROJ_TN,
                             tok_w=tok_w)
            tok = _sb_attention(qkv, batch=batch, seq=seq, heads=heads, tq=SB_TQ, tk=SB_TK,
                                hps=SB_HEADS_PER_STEP)
            xs = _sb_outproj(xs, tok, qkv, kmem, vmem, xq_norm, w_out_b, layer, j_layer,
                             tm=OUT_TM, seq=seq, tok_w=tok_w)
            xs, w_ffn = ffn(xs, ffn2_norm, *w_ffn, layer, cast_next=after_ffn2)
        else:
            xc, w_ffn = ffn(xs, ffn1_norm, *w_ffn, layer, layout="to_chunked",
                            cast_next=after_ffn1)
            u4, qm = _s5_inproj(xc, mix_norm, s5_wu_t, s5_wq, layer, j_layer)
            y4 = _s5_core(u4, ldt, arc, aic, s5_b_re, s5_b_im, s5_c_re, s5_c_im, dcol,
                          expand, tile, j_layer, n_chunks=n_chunks, gps=S5_GROUPS_PER_STEP)
            xc = _s5_outproj(xc, y4, qm, kmem, vmem, xq_norm, s5_wglu_t, w_out_t, layer, j_layer,
                             n_chunks=n_chunks, steps=S5_OUT_STEPS)
            xs, w_ffn = ffn(xc, ffn2_norm, *w_ffn, layer, layout="from_chunked",
                            cast_next=after_ffn2)
    return xs.reshape(batch, seq, d)
```

```python
import functools
import math

import jax
import jax.numpy as jnp
from jax import lax
from jax.experimental import pallas as pl
from jax.experimental.pallas import tpu as pltpu

F32 = jnp.float32
BF16 = jnp.bfloat16

EPS = 1e-6
HEAD_DIM = 128
MEM_HEADS = 4
MEM_W = MEM_HEADS * HEAD_DIM
S5_GROUP = 16
S5_STATE = 64
S5_CHUNK = 64
SWAP_SLAB = 256
F32_EXP2_UNDERFLOW = 151.0

V7X_VMEM_LIMIT_BYTES = 62 * 1024 * 1024


def _cparams(*sem):
    return pltpu.CompilerParams(dimension_semantics=sem, vmem_limit_bytes=V7X_VMEM_LIMIT_BYTES)


def _resident(block_shape, index_map):
    return pl.BlockSpec(block_shape, index_map, pipeline_mode=pl.Buffered(1))


def _rms(x, gain):
    return x * lax.rsqrt(jnp.mean(x * x, axis=-1, keepdims=True) + EPS) * gain


def _dot(a, b):
    return jnp.dot(a, b, preferred_element_type=F32)


def _dot_nt(a, b):
    return lax.dot_general(a, b, (((1,), (1,)), ((), ())), preferred_element_type=F32)


def _split3(x):
    h = x.astype(BF16)
    r = x - h.astype(F32)
    m = r.astype(BF16)
    l = (r - m.astype(F32)).astype(BF16)
    return h, m, l


def _dot_sel(x, sel):
    h, m, l = _split3(x)
    return _dot(h, sel) + _dot(m, sel) + _dot(l, sel)


def _dot3(a, b):
    ah = a.astype(BF16)
    al = (a - ah.astype(F32)).astype(BF16)
    bh = b.astype(BF16)
    bl = (b - bh.astype(F32)).astype(BF16)
    return _dot(ah, bh) + _dot(ah, bl) + _dot(al, bh)


def _ffn_body(*refs, layout, chunk, cast_next):
    n_extra = 2 if cast_next else 0
    x_ref, gain_ref, wg_ref, wu_ref, wd_ref = refs[:5]
    next_in = refs[5:5 + n_extra]
    o_ref = refs[5 + n_extra]
    next_out = refs[6 + n_extra:6 + 2 * n_extra]
    h_ref = refs[6 + 2 * n_extra]

    j = pl.program_id(1)
    tm, d = h_ref.shape
    t = chunk
    cpb = tm // chunk
    slabs = [slice(s, s + SWAP_SLAB) for s in range(0, d, SWAP_SLAB)]

    @pl.when(j == 0)
    def _():
        x = x_ref[...].reshape(tm, d)
        o_ref[...] = x.reshape(o_ref.shape)
        h_ref[...] = _rms(x, gain_ref[...]).astype(BF16)

    h = h_ref[...]
    g = _dot(h, wg_ref[...])
    u = _dot(h, wu_ref[...])
    a = (g * jax.nn.sigmoid(g) * (u * 0.5)).astype(BF16)
    o_ref[...] += _dot(a, wd_ref[...]).reshape(o_ref.shape)

    for src, dst in zip(next_in, next_out):
        dst[...] = src[...].astype(BF16)

    if layout != "rows":
        @pl.when(j == pl.num_programs(1) - 1)
        def _():
            for sl in slabs:
                if layout == "to_chunked":
                    rows = o_ref[:, :, sl].reshape(cpb, t, SWAP_SLAB)
                    o_ref[:, :, sl] = jnp.swapaxes(rows, 0, 1)
                else:
                    rows = o_ref[:, sl].reshape(t, cpb, SWAP_SLAB)
                    o_ref[:, sl] = jnp.swapaxes(rows, 0, 1).reshape(tm, SWAP_SLAB)


def _ffn(x, gain, w_gu, w_down, layer, *, tm, tf, layout="rows", chunk=None, cast_next=None):
    d = x.shape[-1]
    m = x.size // d
    f = w_down.shape[0]
    nf = f // tf
    n_i = m // tm
    rows_spec = pl.BlockSpec((tm, d), lambda i, j: (i, 0))
    rows_shape = jax.ShapeDtypeStruct((m, d), F32)
    scratch = [pltpu.VMEM((tm, d), BF16)]
    if layout == "rows":
        x_spec, o_spec, o_shape = rows_spec, rows_spec, rows_shape
    else:
        cpb = tm // chunk
        chunk_spec = pl.BlockSpec((chunk, cpb, d), lambda i, j: (0, i, 0))
        chunk_shape = jax.ShapeDtypeStruct((chunk, m // chunk, d), F32)
        if layout == "to_chunked":
            x_spec, o_spec, o_shape = rows_spec, chunk_spec, chunk_shape
        else:
            x_spec, o_spec, o_shape = chunk_spec, rows_spec, rows_shape
    in_specs = [
        x_spec,
        pl.BlockSpec((None, 1, d), lambda i, j: (layer, 0, 0)),
        pl.BlockSpec((d, tf), lambda i, j: (0, j)),
        pl.BlockSpec((d, tf), lambda i, j: (0, nf + j)),
        pl.BlockSpec((tf, d), lambda i, j: (j, 0)),
    ]
    args = [x, gain, w_gu, w_gu, w_down]
    out_specs, out_shapes = [o_spec], [o_shape]
    if cast_next is not None:
        gu_f32, down_f32, nxt = cast_next
        gu_rows, gu_cols, down_rows = d // n_i, 2 * f // nf, f // (n_i * nf)
        assert gu_rows * n_i == d and gu_cols * nf == 2 * f and down_rows * n_i * nf == f
        in_specs += [pl.BlockSpec((None, gu_rows, gu_cols), lambda i, j: (nxt, i, j)),
                     pl.BlockSpec((None, down_rows, d), lambda i, j: (nxt, i * nf + j, 0))]
        args += [gu_f32, down_f32]
        out_specs += [pl.BlockSpec((gu_rows, gu_cols), lambda i, j: (i, j)),
                      pl.BlockSpec((down_rows, d), lambda i, j: (i * nf + j, 0))]
        out_shapes += [jax.ShapeDtypeStruct((d, 2 * f), BF16), jax.ShapeDtypeStruct((f, d), BF16)]
    outs = pl.pallas_call(
        functools.partial(_ffn_body, layout=layout, chunk=chunk, cast_next=cast_next is not None),
        grid=(n_i, nf),
        in_specs=in_specs,
        out_specs=out_specs,
        out_shape=out_shapes,
        scratch_shapes=scratch,
        compiler_params=_cparams("parallel", "arbitrary"),
        name="ffn_" + layout,
    )(*args)
    return outs[0], tuple(outs[1:])


def _sb_inproj_body(x_ref, gain_ref, w_ref, o_ref, h_ref, *, n_q_blocks, q_scale):
    j = pl.program_id(1)

    @pl.when(j == 0)
    def _():
        h_ref[...] = _rms(x_ref[...], gain_ref[...]).astype(BF16)

    acc = _dot(h_ref[...], w_ref[...])
    scale = jnp.where(j < n_q_blocks, q_scale, 1.0).astype(F32)
    o_ref[...] = (acc * scale).astype(o_ref.dtype)


def _sb_inproj(x, gain, w_in, layer, j_layer, *, tm, tn, tok_w):
    m, d = x.shape
    n = w_in.shape[2]
    body = functools.partial(_sb_inproj_body, n_q_blocks=tok_w // tn,
                             q_scale=math.log2(math.e) / math.sqrt(HEAD_DIM))
    return pl.pallas_call(
        body,
        grid=(m // tm, n // tn),
        in_specs=[
            pl.BlockSpec((tm, d), lambda i, j: (i, 0)),
            pl.BlockSpec((None, 1, d), lambda i, j: (layer, 0, 0)),
            pl.BlockSpec((None, d, tn), lambda i, j: (j_layer, 0, j)),
        ],
        out_specs=pl.BlockSpec((tm, tn), lambda i, j: (i, j)),
        out_shape=jax.ShapeDtypeStruct((m, n), BF16),
        scratch_shapes=[pltpu.VMEM((tm, d), BF16)],
        compiler_params=_cparams("parallel", "arbitrary"),
        name="sb_inproj",
    )(x, gain, w_in)


def _sb_attn_body(q_ref, k_ref, v_ref, o_ref, acc_ref, carry_ref, *, tq, tk, hps):
    qi = pl.program_id(2)
    diag_blocks = tq // tk
    row = lax.broadcasted_iota(jnp.int32, (tk, tk), 0)
    col = lax.broadcasted_iota(jnp.int32, (tk, tk), 1)
    suffix = (row > col).astype(BF16)

    def visit(kb, diag_index):
        start = pl.multiple_of(kb * tk, tk)
        heads = [slice(hd * HEAD_DIM, (hd + 1) * HEAD_DIM) for hd in range(hps)]
        row0 = 0 if diag_index is None else diag_index * tk
        rows = slice(row0, tq)
        n_rows = tq - row0
        if diag_index is not None:
            before = (lax.broadcasted_iota(jnp.int32, (n_rows, tk), 1)
                      < lax.broadcasted_iota(jnp.int32, (n_rows, tk), 0))
        zs, sps = [], []
        for sl in heads:
            z = _dot_nt(q_ref[rows, sl], k_ref[pl.ds(start, tk), sl])
            neg_abs = lax.bitcast_convert_type(
                lax.bitcast_convert_type(z, jnp.uint32) | jnp.uint32(0x80000000), F32)
            sp = jnp.maximum(z, 0.0) + jnp.log2(1.0 + jnp.exp2(neg_abs))
            if diag_index is not None:
                sp = jnp.where(before, sp, 0.0)
            zs.append(z)
            sps.append(sp)
        spbs = [sp.astype(BF16) for sp in sps]
        inner_all = _dot(jnp.concatenate(spbs, axis=0), suffix)
        for hd, sl in enumerate(heads):
            inner = inner_all[hd * n_rows:(hd + 1) * n_rows]
            total = inner[:, 0:1] + spbs[hd][:, 0:1].astype(F32)
            carry = carry_ref[rows, sl]
            carry_k = jnp.concatenate([carry] * (tk // HEAD_DIM), axis=1)
            w = jnp.exp2((zs[hd] - sps[hd]) - inner - carry_k)
            if diag_index is not None:
                w = jnp.where(before, w, 0.0)
            acc_ref[rows, sl] += _dot(w.astype(BF16), v_ref[pl.ds(start, tk), sl])
            carry_ref[rows, sl] = carry + jnp.broadcast_to(total, carry.shape)

    acc_ref[...] = jnp.zeros_like(acc_ref)
    carry_ref[...] = jnp.zeros_like(carry_ref)
    for j in reversed(range(diag_blocks)):
        visit(qi * diag_blocks + j, j)

    n_far = qi * diag_blocks

    def more(state):
        i, least_carry = state
        return jnp.logical_and(i < n_far, least_carry < F32_EXP2_UNDERFLOW)

    def step(state):
        i, _ = state
        visit(n_far - 1 - i, None)
        return i + 1, jnp.min(carry_ref[...])

    lax.while_loop(more, step, (jnp.int32(0), jnp.min(carry_ref[...])))
    o_ref[...] = acc_ref[...].astype(o_ref.dtype)


def _sb_attention(qkv, *, batch, seq, heads, tq, tk, hps):
    nq = seq // tq
    hg = heads // hps
    w = hps * HEAD_DIM
    body = functools.partial(_sb_attn_body, tq=tq, tk=tk, hps=hps)
    return pl.pallas_call(
        body,
        grid=(batch, hg, nq),
        in_specs=[
            pl.BlockSpec((tq, w), lambda b, h, i: (b * nq + i, h)),
            pl.BlockSpec((seq, w), lambda b, h, i: (b, hg + h)),
            pl.BlockSpec((seq, w), lambda b, h, i: (b, 2 * hg + h)),
        ],
        out_specs=pl.BlockSpec((tq, w), lambda b, h, i: (b * nq + i, h)),
        out_shape=jax.ShapeDtypeStruct((batch * seq, heads * HEAD_DIM), BF16),
        scratch_shapes=[pltpu.VMEM((tq, w), F32), pltpu.VMEM((tq, w), F32)],
        compiler_params=_cparams("parallel", "parallel", "arbitrary"),
        name="sb_attention",
    )(qkv, qkv, qkv)


def _mem_kv_body(mem_ref, gain_ref, w_ref, kgain_ref, k_ref, v_ref):
    h = _rms(mem_ref[...], gain_ref[...]).astype(BF16)
    kv = _dot(h, w_ref[...])
    for hd in range(MEM_HEADS):
        sl = slice(hd * HEAD_DIM, (hd + 1) * HEAD_DIM)
        k_ref[:, sl] = _rms(kv[:, sl], kgain_ref[...]).astype(BF16)
    v_ref[...] = kv[:, MEM_W:].astype(BF16)


def _mem_kv(mem, gain, w_kv, kgain, layer):
    rows, d = mem.shape
    return pl.pallas_call(
        _mem_kv_body,
        grid=(1,),
        in_specs=[
            pl.BlockSpec((rows, d), lambda i: (0, 0)),
            pl.BlockSpec((None, 1, d), lambda i: (layer, 0, 0)),
            pl.BlockSpec((None, d, 2 * MEM_W), lambda i: (layer, 0, 0)),
            pl.BlockSpec((None, 1, HEAD_DIM), lambda i: (layer, 0, 0)),
        ],
        out_specs=[pl.BlockSpec((rows, MEM_W), lambda i: (0, 0))] * 2,
        out_shape=[jax.ShapeDtypeStruct((rows, MEM_W), BF16)] * 2,
        compiler_params=_cparams("arbitrary"),
        name="mem_kv",
    )(mem, gain, w_kv, kgain)


def _cross_attention(qm, k, v, qgain):
    outs = []
    for hd in range(MEM_HEADS):
        sl = slice(hd * HEAD_DIM, (hd + 1) * HEAD_DIM)
        qn = _rms(qm[:, sl], qgain).astype(BF16)
        s = _dot_nt(qn, k[:, sl]) * (1.0 / math.sqrt(HEAD_DIM))
        p = jnp.exp(s - jnp.max(s, axis=-1, keepdims=True))
        p = p / jnp.sum(p, axis=-1, keepdims=True)
        outs.append(_dot(p.astype(BF16), v[:, sl]))
    return jnp.concatenate(outs, axis=1)


def _sb_outproj_body(x_ref, tok_ref, qm_ref, k_ref, v_ref, qgain_ref, w1_ref, w2_ref, o_ref):
    cross = _cross_attention(qm_ref[...].astype(F32), k_ref[...], v_ref[...], qgain_ref[...])
    o_ref[...] = (x_ref[...] + _dot(tok_ref[...], w1_ref[...])
                  + _dot(cross.astype(BF16), w2_ref[...]))


def _sb_outproj(x, tok, qkv, kmem, vmem, qgain, w_out, layer, j_layer, *, tm, seq, tok_w):
    m, d = x.shape
    n_mem = kmem.shape[0] // (m // seq)
    per_batch = seq // tm
    qm_block = tok_w * 3 // MEM_W
    return pl.pallas_call(
        _sb_outproj_body,
        grid=(m // tm,),
        in_specs=[
            pl.BlockSpec((tm, d), lambda i: (i, 0)),
            pl.BlockSpec((tm, tok_w), lambda i: (i, 0)),
            pl.BlockSpec((tm, MEM_W), lambda i: (i, qm_block)),
            pl.BlockSpec((n_mem, MEM_W), lambda i: (i // per_batch, 0)),
            pl.BlockSpec((n_mem, MEM_W), lambda i: (i // per_batch, 0)),
            pl.BlockSpec((None, 1, HEAD_DIM), lambda i: (layer, 0, 0)),
            _resident((None, tok_w, d), lambda i: (j_layer, 0, 0)),
            _resident((None, MEM_W, d), lambda i: (j_layer, tok_w // MEM_W, 0)),
        ],
        out_specs=pl.BlockSpec((tm, d), lambda i: (i, 0)),
        out_shape=jax.ShapeDtypeStruct((m, d), F32),
        compiler_params=_cparams("parallel"),
        name="sb_outproj",
    )(x, tok, qkv, kmem, vmem, qgain, w_out, w_out)


def _s5_inproj_body(x_ref, gain_ref, wut_ref, wq_ref, u_ref, qm_ref):
    h = _rms(x_ref[...], gain_ref[...]).astype(BF16)
    ut = _dot_nt(wut_ref[...], h)
    u_ref[...] = ut.reshape(u_ref.shape)
    qm_ref[0] = _dot(h, wq_ref[...]).astype(qm_ref.dtype)


def _s5_inproj(xc, gain, wut, wq, layer, j_layer):
    t, n, d = xc.shape
    tok_w = wut.shape[1]
    groups = tok_w // S5_GROUP
    return pl.pallas_call(
        _s5_inproj_body,
        grid=(t,),
        in_specs=[
            pl.BlockSpec((None, n, d), lambda i: (i, 0, 0)),
            pl.BlockSpec((None, 1, d), lambda i: (layer, 0, 0)),
            _resident((None, tok_w, d), lambda i: (j_layer, 0, 0)),
            _resident((None, d, MEM_W), lambda i: (j_layer, 0, 0)),
        ],
        out_specs=[
            pl.BlockSpec((groups, 1, S5_GROUP, n), lambda i: (0, i, 0, 0)),
            pl.BlockSpec((1, n, MEM_W), lambda i: (i, 0, 0)),
        ],
        out_shape=[
            jax.ShapeDtypeStruct((groups, t, S5_GROUP, n), F32),
            jax.ShapeDtypeStruct((t, n, MEM_W), BF16),
        ],
        compiler_params=_cparams("parallel"),
        name="s5_inproj",
    )(xc, gain, wut, wq)


def _cexp(zr, zi, k):
    mag = jnp.exp(zr * k)
    return mag * jnp.cos(zi * k), mag * jnp.sin(zi * k)


def _s5_core_body(u_ref, ldt_ref, arc_ref, aic_ref, br_ref, bi_ref,
                  cr_ref, ci_ref, d_ref, exp_ref, tile_ref, y_ref, mt_ref, q_ref, *, n_chunks):
    for g in range(u_ref.shape[0]):
        _s5_core_group(g, u_ref, ldt_ref, arc_ref, aic_ref, br_ref, bi_ref, cr_ref, ci_ref,
                       d_ref, exp_ref, tile_ref, y_ref, mt_ref, q_ref, n_chunks=n_chunks)


def _s5_core_group(g, u_ref, ldt_ref, arc_ref, aic_ref, br_ref, bi_ref,
                   cr_ref, ci_ref, d_ref, exp_ref, tile_ref, y_ref, mt_ref, q_ref, *, n_chunks):
    t = S5_CHUNK
    c = S5_GROUP
    ns = S5_STATE
    n = u_ref.shape[-1]
    dt = jnp.exp(ldt_ref[g])

    lr, li = arc_ref[g], aic_ref[g]
    zr, zi = lr * dt, li * dt
    back = (t - 1 - lax.broadcasted_iota(jnp.int32, (ns, t), 1)).astype(F32)
    pwr, pwi = _cexp(zr, zi, back)
    abr, abi = pwr[:, t - 2:t - 1], pwi[:, t - 2:t - 1]
    top_r, top_i = pwr[:, 0:1], pwi[:, 0:1]
    mr, mi = top_r * abr - top_i * abi, top_r * abi + top_i * abr
    pwr_t, pwi_t = pwr.T, pwi.T

    nr, ni = abr - 1.0, abi
    den = lr * lr + li * li
    fr = (nr * lr + ni * li) / den
    fi = (ni * lr - nr * li) / den
    b_re, b_im = br_ref[g], bi_ref[g]
    bbr = fr * b_re - fi * b_im
    bbi = fr * b_im + fi * b_re

    pwr_x, pwi_x = _dot_sel(pwr, exp_ref[...]), _dot_sel(pwi, exp_ref[...])
    bbr_x, bbi_x = _dot_sel(bbr, tile_ref[...]), _dot_sel(bbi, tile_ref[...])
    p_re = pwr_x * bbr_x - pwi_x * bbi_x
    p_im = pwr_x * bbi_x + pwi_x * bbr_x
    p = jnp.concatenate([p_re, p_im], axis=0)

    c_re, c_im = cr_ref[g], ci_ref[g]
    c_ext = jnp.concatenate([c_re, -c_im], axis=1)
    k_rev = jnp.concatenate([_dot3(c_ext, p), jnp.zeros((c, t * c), F32)], axis=1)
    width = 2 * t * c
    for step in range(t):
        off = (t - 1 - step) * c
        win = k_rev if off == 0 else pltpu.roll(k_rev, width - off, 1)
        mt_ref[g, step * c:(step + 1) * c, :] = win[:, :t * c].astype(BF16)

    for step in range(t):
        if step < t - 1:
            src = t - 2 - step
            ar, ai = pwr_t[src:src + 1, :], pwi_t[src:src + 1, :]
        else:
            hr, hi, lr_, li_ = pwr_t[0:1], pwi_t[0:1], pwr_t[t - 2:t - 1], pwi_t[t - 2:t - 1]
            ar, ai = hr * lr_ - hi * li_, hr * li_ + hi * lr_
        q_ref[g, step * c:(step + 1) * c, 0:ns] = (c_re * ar - c_im * ai).astype(BF16)
        q_ref[g, step * c:(step + 1) * c, ns:2 * ns] = (-(c_re * ai + c_im * ar)).astype(BF16)

    u = u_ref[g].reshape(t * c, n)
    ub = u.astype(BF16)
    inj = _dot(p.astype(BF16), ub)
    s_re, s_im = inj[:ns], inj[ns:]

    pos = lax.broadcasted_iota(jnp.int32, (ns, n), 1) % n_chunks
    shift = 1
    while shift < n_chunks:
        keep = pos >= shift
        sh_re = jnp.where(keep, pltpu.roll(s_re, shift, 1), 0.0)
        sh_im = jnp.where(keep, pltpu.roll(s_im, shift, 1), 0.0)
        s_re, s_im = s_re + mr * sh_re - mi * sh_im, s_im + mr * sh_im + mi * sh_re
        mr, mi = mr * mr - mi * mi, 2.0 * mr * mi
        shift *= 2
    first = pos >= 1
    prev = jnp.concatenate([jnp.where(first, pltpu.roll(s_re, 1, 1), 0.0),
                            jnp.where(first, pltpu.roll(s_im, 1, 1), 0.0)], axis=0)

    dcol = jnp.concatenate([d_ref[g]] * t, axis=0)
    y = _dot(mt_ref[g], ub) + _dot(q_ref[g], prev.astype(BF16)) + dcol * u
    y_ref[g] = jax.nn.gelu(y).reshape(t, c, n)


def _s5_core(u4, ldt, arc, aic, b_re, b_im, c_re, c_im, dcol, expand, tile, j_layer, *, n_chunks,
             gps):
    groups, t, c, n = u4.shape
    ns = S5_STATE
    body = functools.partial(_s5_core_body, n_chunks=n_chunks)

    def per_group(shape):
        return pl.BlockSpec((None, gps) + shape, lambda g: (j_layer, g, 0, 0))

    return pl.pallas_call(
        body,
        grid=(groups // gps,),
        in_specs=[
            pl.BlockSpec((gps, t, c, n), lambda g: (g, 0, 0, 0)),
            per_group((1, 1)),
            per_group((ns, 1)), per_group((ns, 1)),
            per_group((ns, c)), per_group((ns, c)),
            per_group((c, ns)), per_group((c, ns)),
            per_group((c, 1)),
            pl.BlockSpec((t, t * c), lambda g: (0, 0)),
            pl.BlockSpec((c, t * c), lambda g: (0, 0)),
        ],
        out_specs=pl.BlockSpec((gps, t, c, n), lambda g: (g, 0, 0, 0)),
        out_shape=jax.ShapeDtypeStruct((groups, t, c, n), F32),
        scratch_shapes=[pltpu.VMEM((gps, t * c, t * c), BF16),
                        pltpu.VMEM((gps, t * c, 2 * ns), BF16)],
        compiler_params=_cparams("parallel"),
        name="s5_core",
    )(u4, ldt, arc, aic, b_re, b_im, c_re, c_im, dcol, expand, tile)


def _s5_cross_body(qm_ref, k_ref, v_ref, qgain_ref, o_ref, *, n_mem, nc):
    steps, n, w = qm_ref.shape
    for i in range(n // nc):
        qm = qm_ref[:, i * nc:(i + 1) * nc, :].astype(F32).reshape(steps * nc, w)
        cross = _cross_attention(qm, k_ref[i * n_mem:(i + 1) * n_mem],
                                 v_ref[i * n_mem:(i + 1) * n_mem], qgain_ref[...])
        for s in range(steps):
            o_ref[s, :, i * nc:(i + 1) * nc] = cross[s * nc:(s + 1) * nc].T.astype(o_ref.dtype)


def _s5_cross(qm, kmem, vmem, qgain, layer, *, n_chunks, steps):
    t, n, w = qm.shape
    n_mem = kmem.shape[0] // (n // n_chunks)
    return pl.pallas_call(
        functools.partial(_s5_cross_body, n_mem=n_mem, nc=n_chunks),
        grid=(t // steps,),
        in_specs=[
            pl.BlockSpec((steps, n, w), lambda i: (i, 0, 0)),
            _resident(kmem.shape, lambda i: (0, 0)),
            _resident(vmem.shape, lambda i: (0, 0)),
            pl.BlockSpec((None, 1, HEAD_DIM), lambda i: (layer, 0, 0)),
        ],
        out_specs=pl.BlockSpec((steps, w, n), lambda i: (i, 0, 0)),
        out_shape=jax.ShapeDtypeStruct((t, w, n), BF16),
        compiler_params=_cparams("parallel"),
        name="s5_cross",
    )(qm, kmem, vmem, qgain)


def _s5_outproj_body(x_ref, y_ref, cross_ref, wglut_ref, woutt_ref, o_ref):
    steps, n, d = x_ref.shape
    groups, _, c, _ = y_ref.shape
    yts = [y_ref[:, s].reshape(groups * c, n) for s in range(steps)]
    gates = [_dot(wglut_ref[...], yt.astype(BF16)) for yt in yts]
    mixed = [jnp.concatenate([(yt * jax.nn.sigmoid(g)).astype(BF16), cross_ref[s]], axis=0)
             for s, (yt, g) in enumerate(zip(yts, gates))]
    for s in range(steps):
        o_ref[s] = x_ref[s] + _dot(woutt_ref[...], mixed[s]).T


def _s5_outproj(xc, y4, cross, wglut, woutt, j_layer, *, steps):
    t, n, d = xc.shape
    groups, _, c, _ = y4.shape
    tok_w = groups * c
    return pl.pallas_call(
        _s5_outproj_body,
        grid=(t // steps,),
        in_specs=[
            pl.BlockSpec((steps, n, d), lambda i: (i, 0, 0)),
            pl.BlockSpec((groups, steps, c, n), lambda i: (0, i, 0, 0)),
            pl.BlockSpec((steps, MEM_W, n), lambda i: (i, 0, 0)),
            _resident((None, tok_w, tok_w), lambda i: (j_layer, 0, 0)),
            _resident((None, d, d), lambda i: (j_layer, 0, 0)),
        ],
        out_specs=pl.BlockSpec((steps, n, d), lambda i: (i, 0, 0)),
        out_shape=jax.ShapeDtypeStruct(xc.shape, F32),
        compiler_params=_cparams("parallel"),
        name="s5_outproj",
    )(xc, y4, cross, wglut, woutt)


FFN_TM = 1024
FFN_TF = 512
PROJ_TM = 2048
PROJ_TN = 512
OUT_TM = 512
S5_OUT_STEPS = 2
S5_CROSS_STEPS = 8
S5_GROUPS_PER_STEP = 2
SB_TQ = 512
SB_TK = 256
SB_HEADS_PER_STEP = 4


def kernel(x, mem, ffn1_norm, ffn1_w_gu, ffn1_w_down, mix_norm, mem_norm, w_mem_kv, xq_norm,
           xk_norm, w_out, ffn2_norm, ffn2_w_gu, ffn2_w_down, sb_w_in, s5_w_in, s5_log_dt,
           s5_a_re, s5_a_im, s5_b_re, s5_b_im, s5_c_re, s5_c_im, s5_d, s5_w_glu):
    batch, seq, d = x.shape
    depth = ffn1_norm.shape[0]
    n_mem = mem.shape[1]
    tok_w = d - MEM_W
    heads = tok_w // HEAD_DIM
    groups = tok_w // S5_GROUP
    n_chunks = seq // S5_CHUNK
    n_b = s5_w_in.shape[0]
    ns, c, t = S5_STATE, S5_GROUP, S5_CHUNK

    def row(g):
        return g.reshape(g.shape[0], 1, g.shape[1])

    ffn1_norm, ffn2_norm, mix_norm, mem_norm = map(row, (ffn1_norm, ffn2_norm, mix_norm, mem_norm))
    xq_norm, xk_norm = row(xq_norm), row(xk_norm)
    w_kv = w_mem_kv.astype(BF16)
    w_out_b = w_out[0::2].astype(BF16)
    w_out_t = jnp.swapaxes(w_out[1::2], 1, 2).astype(BF16)
    sb_w = sb_w_in.astype(BF16)
    s5_wu_t = jnp.swapaxes(s5_w_in[:, :, :tok_w], 1, 2).astype(BF16)
    s5_wq = s5_w_in[:, :, tok_w:].astype(BF16)
    s5_wglu_t = jnp.swapaxes(s5_w_glu, 1, 2).astype(BF16)

    ldt = s5_log_dt.reshape(n_b, groups, 1, 1)
    arc, aic = s5_a_re.reshape(n_b, groups, ns, 1), s5_a_im.reshape(n_b, groups, ns, 1)
    dcol = s5_d.reshape(n_b, groups, c, 1)
    lane = jnp.arange(t * c)
    expand = (lane[None, :] // c == jnp.arange(t)[:, None]).astype(BF16)
    tile = (lane[None, :] % c == jnp.arange(c)[:, None]).astype(BF16)

    mem2 = mem.reshape(batch * n_mem, d)
    xs = x.reshape(batch * seq, d)
    def ffn(xin, gain, w_gu, w_down, layer, layout="rows", cast_next=None):
        tm = min(FFN_TM, batch * seq)
        return _ffn(xin, gain, w_gu, w_down, layer, tm=tm, tf=FFN_TF, layout=layout, chunk=t,
                    cast_next=cast_next)

    w_ffn = (ffn1_w_gu[0].astype(BF16), ffn1_w_down[0].astype(BF16))
    for layer in range(depth):
        kmem, vmem = _mem_kv(mem2, mem_norm, w_kv, xk_norm, layer)
        j_layer = layer // 2
        after_ffn1 = (ffn2_w_gu, ffn2_w_down, layer)
        after_ffn2 = (ffn1_w_gu, ffn1_w_down, layer + 1) if layer + 1 < depth else None
        if layer % 2 == 0:
            xs, w_ffn = ffn(xs, ffn1_norm, *w_ffn, layer, cast_next=after_ffn1)
            qkv = _sb_inproj(xs, mix_norm, sb_w, layer, j_layer,
                             tm=min(PROJ_TM, batch * seq), tn=PROJ_TN,
                             tok_w=tok_w)
            tok = _sb_attention(qkv, batch=batch, seq=seq, heads=heads, tq=SB_TQ, tk=SB_TK,
                                hps=SB_HEADS_PER_STEP)
            xs = _sb_outproj(xs, tok, qkv, kmem, vmem, xq_norm, w_out_b, layer, j_layer,
                             tm=OUT_TM, seq=seq, tok_w=tok_w)
            xs, w_ffn = ffn(xs, ffn2_norm, *w_ffn, layer, cast_next=after_ffn2)
        else:
            xc, w_ffn = ffn(xs, ffn1_norm, *w_ffn, layer, layout="to_chunked",
                            cast_next=after_ffn1)
            u4, qm = _s5_inproj(xc, mix_norm, s5_wu_t, s5_wq, layer, j_layer)
            y4 = _s5_core(u4, ldt, arc, aic, s5_b_re, s5_b_im, s5_c_re, s5_c_im, dcol,
                          expand, tile, j_layer, n_chunks=n_chunks, gps=S5_GROUPS_PER_STEP)
            cross = _s5_cross(qm, kmem, vmem, xq_norm, layer, n_chunks=n_chunks,
                              steps=S5_CROSS_STEPS)
            xc = _s5_outproj(xc, y4, cross, s5_wglu_t, w_out_t, j_layer, steps=S5_OUT_STEPS)
            xs, w_ffn = ffn(xc, ffn2_norm, *w_ffn, layer, layout="from_chunked",
                            cast_next=after_ffn2)
    return xs.reshape(batch, seq, d)
```
